```python
import math
import jax, jax.numpy as jnp
from jax import lax
import numpy as np

D_MODEL = 1024
BATCH = 8
SEQ = 4096
DEPTH = 2

HEAD_DIM = 64
MIX_W = D_MODEL // 4
RWKV_W = MIX_W
RWKV_HEADS = RWKV_W // HEAD_DIM
RWKV_W_LORA = 64
RWKV_A_LORA = 64
RWKV_LNX_EPS = 64e-5
DSA_W = MIX_W
DSA_HEADS = DSA_W // HEAD_DIM
IDX_HEADS = 4
IDX_DIM = 64
TOPK_MAX = 256
Q_BLOCK = 128
RET_W = MIX_W
RET_HEADS = RET_W // HEAD_DIM
RET_CHUNK = 128
S5_W = MIX_W
S5_GROUP = 16
S5_GROUPS = S5_W // S5_GROUP
S5_STATE = 64
XATT_W = MIX_W
XATT_HEADS = XATT_W // HEAD_DIM
N_MEM = 256
N_BRANCH = 5
ROPE_THETA = 10000.0
NORM_EPS = 1e-6

RWKV_SIZES = (RWKV_W, RWKV_W, RWKV_W, RWKV_W_LORA, RWKV_A_LORA, RWKV_W)
DSA_SIZES = (DSA_W, DSA_W, DSA_W, IDX_HEADS * IDX_DIM, IDX_DIM, IDX_HEADS, DSA_W)
RET_SIZES = (RET_W, RET_W, RET_W, RET_W)
S5_SIZES = (S5_W, S5_W)
XATT_SIZES = (XATT_W, XATT_W)
RWKV_COLS = 4 * RWKV_W + RWKV_W_LORA + RWKV_A_LORA
DSA_COLS = 4 * DSA_W + IDX_HEADS * IDX_DIM + IDX_DIM + IDX_HEADS
RET_COLS = 4 * RET_W
S5_COLS = 2 * S5_W
XATT_COLS = 2 * XATT_W
GATE_COLS = N_BRANCH * D_MODEL
IN_SIZES = (RWKV_COLS, DSA_COLS, RET_COLS, S5_COLS, XATT_COLS, GATE_COLS)
N_IN = RWKV_COLS + DSA_COLS + RET_COLS + S5_COLS + XATT_COLS + GATE_COLS

kernel_name = 'hybrid_gated_rwkv7_dsa_retention_s5_block'


def _split(t, sizes):
    out, start = [], 0
    for s in sizes:
        out.append(t[..., start:start + s])
        start += s
    return out


def rms_norm(x, g):
    xf = x.astype(jnp.float32)
    y = xf * lax.rsqrt(jnp.mean(xf * xf, -1, keepdims=True) + NORM_EPS)
    return (y * g.astype(jnp.float32)).astype(x.dtype)


def head_norm(y, w, b, eps):
    bsz, seq, nh, hd = y.shape
    y = y.astype(jnp.float32)
    mu = jnp.mean(y, -1, keepdims=True)
    var = jnp.mean(jnp.square(y - mu), -1, keepdims=True)
    y = ((y - mu) * lax.rsqrt(var + eps)).reshape(bsz, seq, nh * hd) * w
    return y if b is None else y + b


def rope(x, pos):
    half = x.shape[-1] // 2
    inv = ROPE_THETA ** (-jnp.arange(half, dtype=jnp.float32) / half)
    ang = pos.astype(jnp.float32)[..., None] * inv
    cos, sin = jnp.cos(ang)[:, :, None, :], jnp.sin(ang)[:, :, None, :]
    xf = x.astype(jnp.float32)
    x1, x2 = xf[..., :half], xf[..., half:]
    return jnp.concatenate([x1 * cos - x2 * sin, x2 * cos + x1 * sin], -1)


def rwkv7_branch(c, mu, w0, w2, a0, a2, k_k, k_a, r_k, lnx_w, lnx_b):
    bsz, seq, _ = c.shape
    c = c.astype(jnp.float32)
    prev = jnp.pad(c, ((0, 0), (1, 0), (0, 0)))[:, :-1]
    c = c + mu * (prev - c)
    r, k, v, wl, al, g = _split(c, RWKV_SIZES)
    w_log = -jax.nn.softplus(-(w0 + jnp.tanh(wl) @ w2)) - 0.5
    decay = jnp.exp(-jnp.exp(w_log))
    a = jax.nn.sigmoid(a0 + al @ a2)
    hs = lambda t: t.reshape(bsz, seq, RWKV_HEADS, HEAD_DIM)
    kk = hs(k * k_k)
    kk = kk / jnp.maximum(jnp.sqrt(jnp.sum(kk * kk, -1, keepdims=True)), 1e-12)
    k = k * (1.0 + (a - 1.0) * k_a)
    r, k, v, decay, a = hs(r), hs(k), hs(v), hs(decay), hs(a)

    def step(state, inp):
        r_t, w_t, k_t, v_t, kk_t, a_t = inp
        sa = jnp.einsum('bhvk,bhk->bhv', state, -kk_t)
        state = (state * w_t[:, :, None, :] + sa[..., None] * (kk_t * a_t)[:, :, None, :]
                 + v_t[..., None] * k_t[:, :, None, :])
        return state, jnp.einsum('bhvk,bhk->bhv', state, r_t)

    tm = lambda t: jnp.moveaxis(t, 1, 0)
    s0 = jnp.zeros((bsz, RWKV_HEADS, HEAD_DIM, HEAD_DIM), jnp.float32)
    _, y = lax.scan(step, s0, (tm(r), tm(decay), tm(k), tm(v), tm(kk), tm(a)))
    y = head_norm(jnp.moveaxis(y, 0, 1), lnx_w, lnx_b, RWKV_LNX_EPS)
    bonus = jnp.sum(r * k * r_k, -1, keepdims=True) * v
    return (y + bonus.reshape(bsz, seq, RWKV_W)) * jax.nn.silu(g)


def dsa_branch(c, pos):
    bsz, seq, _ = c.shape
    q, k, v, iq, ik, iw, g = _split(c, DSA_SIZES)
    q = rope(q.reshape(bsz, seq, DSA_HEADS, HEAD_DIM), pos)
    k = rope(k.reshape(bsz, seq, DSA_HEADS, HEAD_DIM), pos)
    v = v.reshape(bsz, seq, DSA_HEADS, HEAD_DIM).astype(jnp.float32)
    iq = rope(iq.reshape(bsz, seq, IDX_HEADS, IDX_DIM), pos) * IDX_DIM ** -0.5
    ik = rope(ik[:, :, None, :], pos)[:, :, 0]
    iw = iw.astype(jnp.float32) * IDX_HEADS ** -0.5
    n_sel = min(TOPK_MAX, seq // 4)
    n_blk = seq // Q_BLOCK
    key_idx = jnp.arange(seq)

    def block(args):
        qb, iqb, iwb, tb = args
        s_idx = jnp.einsum('bqhd,bsd->bqhs', iqb, ik)
        score = jnp.einsum('bqhs,bqh->bqs', jax.nn.relu(s_idx), iwb)
        causal = key_idx[None, None, :] <= tb[None, :, None]
        score = jnp.where(causal, score, -jnp.inf)
        _, sel = lax.top_k(score, n_sel)
        gather = jax.vmap(lambda tb_, ib_: tb_[ib_])
        k_sel = gather(k, sel)
        v_sel = gather(v, sel)
        logits = jnp.einsum('bqhd,bqnhd->bhqn', qb, k_sel) * HEAD_DIM ** -0.5
        valid = (sel <= tb[None, :, None])[:, None]
        p = jax.nn.softmax(jnp.where(valid, logits, -1e30), -1)
        return jnp.einsum('bhqn,bqnhd->bqhd', p, v_sel)

    blocks = lambda t: jnp.moveaxis(t.reshape((bsz, n_blk, Q_BLOCK) + t.shape[2:]), 1, 0)
    out = lax.map(block, (blocks(q), blocks(iq), blocks(iw), key_idx.reshape(n_blk, Q_BLOCK)))
    out = jnp.moveaxis(out, 0, 1).reshape(bsz, seq, DSA_W)
    return out * jax.nn.silu(g.astype(jnp.float32))


def retention_branch(c, pos, gn_w):
    bsz, seq, _ = c.shape
    q, k, v, g = _split(c, RET_SIZES)
    q = rope(q.reshape(bsz, seq, RET_HEADS, HEAD_DIM), pos)
    k = rope(k.reshape(bsz, seq, RET_HEADS, HEAD_DIM), pos) * HEAD_DIM ** -0.5
    v = v.reshape(bsz, seq, RET_HEADS, HEAD_DIM).astype(jnp.float32)
    log_g = jnp.log(1.0 - jnp.exp(jnp.linspace(math.log(1.0 / 32), math.log(1.0 / 512), RET_HEADS)))
    n_ch = seq // RET_CHUNK
    j = jnp.arange(RET_CHUNK, dtype=jnp.float32)
    rel = j[:, None] - j[None, :]
    inner_decay = jnp.where(rel >= 0, jnp.exp(log_g[:, None, None] * jnp.maximum(rel, 0.0)), 0.0)
    q_decay = jnp.exp(log_g[:, None] * (j + 1.0))[..., None]
    k_decay = jnp.exp(log_g[:, None] * (RET_CHUNK - 1.0 - j))[..., None]
    chunk_decay = jnp.exp(log_g * RET_CHUNK)[:, None, None]

    def step(R, inp):
        qc, kc, vc = inp
        a = jnp.einsum('bhqd,bhkd->bhqk', qc, kc) * inner_decay
        o = jnp.einsum('bhqk,bhkv->bhqv', a, vc) + jnp.einsum('bhqd,bhdv->bhqv', qc, R) * q_decay
        R = chunk_decay * R + jnp.einsum('bhkd,bhkv->bhdv', kc * k_decay, vc)
        return R, o

    chunks = lambda t: t.reshape(bsz, n_ch, RET_CHUNK, RET_HEADS, HEAD_DIM).transpose(1, 0, 3, 2, 4)
    r0 = jnp.zeros((bsz, RET_HEADS, HEAD_DIM, HEAD_DIM), jnp.float32)
    _, o = lax.scan(step, r0, (chunks(q), chunks(k), chunks(v)))
    o = o.transpose(1, 0, 3, 2, 4).reshape(bsz, seq, RET_HEADS, HEAD_DIM)
    return head_norm(o, gn_w, None, NORM_EPS) * jax.nn.silu(g.astype(jnp.float32))


def _complex_affine_combine(e1, e2):
    a1r, a1i, b1r, b1i = e1
    a2r, a2i, b2r, b2i = e2
    return (a2r * a1r - a2i * a1i, a2r * a1i + a2i * a1r,
            a2r * b1r - a2i * b1i + b2r, a2r * b1i + a2i * b1r + b2i)


def s5_branch(c, lam_re, lam_im, log_dt, b_re, b_im, c_re, c_im, d_skip, w_glu):
    bsz, seq, _ = c.shape
    u, g = _split(c.astype(jnp.float32), S5_SIZES)
    ug = u.reshape(bsz, seq, S5_GROUPS, S5_GROUP)
    lr = jnp.minimum(lam_re.astype(jnp.float32), -1e-4)
    li = lam_im.astype(jnp.float32)
    dt = jnp.exp(log_dt.astype(jnp.float32))[:, None]
    mag = jnp.exp(lr * dt)
    ab_re, ab_im = mag * jnp.cos(li * dt), mag * jnp.sin(li * dt)
    den = lr * lr + li * li
    f_re = ((ab_re - 1.0) * lr + ab_im * li) / den
    f_im = (ab_im * lr - (ab_re - 1.0) * li) / den
    bb_re = f_re[..., None] * b_re - f_im[..., None] * b_im
    bb_im = f_re[..., None] * b_im + f_im[..., None] * b_re
    bu_re = jnp.einsum('gpc,bsgc->bsgp', bb_re, ug)
    bu_im = jnp.einsum('gpc,bsgc->bsgp', bb_im, ug)
    a_re = jnp.broadcast_to(ab_re, bu_re.shape)
    a_im = jnp.broadcast_to(ab_im, bu_im.shape)
    _, _, xs_re, xs_im = lax.associative_scan(_complex_affine_combine, (a_re, a_im, bu_re, bu_im), axis=1)
    y = jnp.einsum('gcp,bsgp->bsgc', c_re, xs_re) - jnp.einsum('gcp,bsgp->bsgc', c_im, xs_im)
    y = y.reshape(bsz, seq, S5_W) + d_skip * u
    y = jax.nn.gelu(y)
    y = y * jax.nn.sigmoid(y @ w_glu)
    return y * jax.nn.silu(g)


def memory_branch(c, mem_kv):
    bsz, seq, _ = c.shape
    q, g = _split(c, XATT_SIZES)
    q = q.reshape(bsz, seq, XATT_HEADS, HEAD_DIM).astype(jnp.float32)
    km, vm = _split(mem_kv.astype(jnp.float32), (XATT_W, XATT_W))
    km = km.reshape(bsz, -1, XATT_HEADS, HEAD_DIM)
    vm = vm.reshape(bsz, -1, XATT_HEADS, HEAD_DIM)
    p = jax.nn.softmax(jnp.einsum('bshd,bmhd->bhsm', q, km) * HEAD_DIM ** -0.5, -1)
    o = jnp.einsum('bhsm,bmhd->bshd', p, vm).reshape(bsz, seq, XATT_W)
    return o * jax.nn.silu(g.astype(jnp.float32))


def setup_inputs(seed: int = 0) -> dict:
    key = jax.random.key(seed)
    ks = jax.random.split(key, 32)
    L, D = DEPTH, D_MODEL
    nrm = lambda k, shape, s: jax.random.normal(k, shape, jnp.float32) * s
    offsets = jax.random.randint(ks[2], (BATCH,), 0, SEQ, dtype=jnp.int32)
    return {
        'x': nrm(ks[0], (BATCH, SEQ, D), 1.0),
        'mem': nrm(ks[1], (BATCH, N_MEM, D), 1.0),
        'positions': offsets[:, None] + jnp.arange(SEQ, dtype=jnp.int32)[None, :],
        'norm_pre': 1.0 + nrm(ks[3], (L, D), 0.05),
        'norm_post': 1.0 + nrm(ks[4], (L, D), 0.05),
        'norm_mem': 1.0 + nrm(ks[5], (L, D), 0.05),
        'w_in': nrm(ks[6], (L, D, N_IN), D ** -0.5),
        'rwkv_mu': jax.random.uniform(ks[7], (L, RWKV_COLS), jnp.float32),
        'rwkv_w0': jax.random.uniform(ks[8], (L, RWKV_W), jnp.float32, -2.0, 1.0),
        'rwkv_w2': nrm(ks[9], (L, RWKV_W_LORA, RWKV_W), 0.5 * RWKV_W_LORA ** -0.5),
        'rwkv_a0': nrm(ks[10], (L, RWKV_W), 0.3),
        'rwkv_a2': nrm(ks[11], (L, RWKV_A_LORA, RWKV_W), 0.5 * RWKV_A_LORA ** -0.5),
        'rwkv_k_k': 0.85 + nrm(ks[12], (L, RWKV_W), 0.05),
        'rwkv_k_a': 1.0 + nrm(ks[13], (L, RWKV_W), 0.05),
        'rwkv_r_k': nrm(ks[14], (L, RWKV_HEADS, HEAD_DIM), 0.3),
        'rwkv_lnx_w': 1.0 + nrm(ks[15], (L, RWKV_W), 0.05),
        'rwkv_lnx_b': nrm(ks[16], (L, RWKV_W), 0.02),
        'ret_gn_w': 1.0 + nrm(ks[17], (L, RET_W), 0.05),
        's5_lam_re': -0.5 + nrm(ks[18], (L, S5_GROUPS, S5_STATE), 0.01),
        's5_lam_im': math.pi * jnp.arange(S5_STATE, dtype=jnp.float32)[None, None, :] + nrm(ks[19], (L, S5_GROUPS, S5_STATE), 0.01),
        's5_log_dt': jax.random.uniform(ks[20], (L, S5_GROUPS), jnp.float32, math.log(1e-3), math.log(1e-1)),
        's5_b_re': nrm(ks[21], (L, S5_GROUPS, S5_STATE, S5_GROUP), (2 * S5_GROUP) ** -0.5),
        's5_b_im': nrm(ks[22], (L, S5_GROUPS, S5_STATE, S5_GROUP), (2 * S5_GROUP) ** -0.5),
        's5_c_re': nrm(ks[23], (L, S5_GROUPS, S5_GROUP, S5_STATE), S5_STATE ** -0.5),
        's5_c_im': nrm(ks[24], (L, S5_GROUPS, S5_GROUP, S5_STATE), S5_STATE ** -0.5),
        's5_d': nrm(ks[25], (L, S5_W), 0.5),
        's5_w_glu': nrm(ks[26], (L, S5_W, S5_W), S5_W ** -0.5),
        'w_mem_kv': nrm(ks[27], (L, D, 2 * XATT_W), D ** -0.5),
        'w_branch': nrm(ks[28], (L, N_BRANCH, MIX_W, D), MIX_W ** -0.5),
        'w_out': nrm(ks[29], (L, D, D), D ** -0.5),
    }


def reference(x, mem, positions, norm_pre, norm_post, norm_mem, w_in, rwkv_mu, rwkv_w0, rwkv_w2, rwkv_a0,
              rwkv_a2, rwkv_k_k, rwkv_k_a, rwkv_r_k, rwkv_lnx_w, rwkv_lnx_b, ret_gn_w, s5_lam_re, s5_lam_im,
              s5_log_dt, s5_b_re, s5_b_im, s5_c_re, s5_c_im, s5_d, s5_w_glu, w_mem_kv, w_branch, w_out):
    bsz, seq, _ = x.shape
    for l in range(DEPTH):
        h = rms_norm(x, norm_pre[l])
        cols = jnp.einsum('bsd,dn->bsn', h, w_in[l])
        c_rwkv, c_dsa, c_ret, c_s5, c_x, c_gate = _split(cols, IN_SIZES)
        mem_kv = jnp.einsum('bmd,dn->bmn', rms_norm(mem, norm_mem[l]), w_mem_kv[l])
        ys = (
            rwkv7_branch(c_rwkv, rwkv_mu[l], rwkv_w0[l], rwkv_w2[l], rwkv_a0[l], rwkv_a2[l], rwkv_k_k[l],
                         rwkv_k_a[l], rwkv_r_k[l], rwkv_lnx_w[l], rwkv_lnx_b[l]),
            dsa_branch(c_dsa, positions),
            retention_branch(c_ret, positions, ret_gn_w[l]),
            s5_branch(c_s5, s5_lam_re[l], s5_lam_im[l], s5_log_dt[l], s5_b_re[l], s5_b_im[l], s5_c_re[l],
                      s5_c_im[l], s5_d[l], s5_w_glu[l]),
            memory_branch(c_x, mem_kv),
        )
        gates = jax.nn.sigmoid(c_gate.astype(jnp.float32)).reshape(bsz, seq, N_BRANCH, D_MODEL)
        merged = None
        for i in range(N_BRANCH):
            term = gates[:, :, i] * jnp.einsum('bsw,wd->bsd', ys[i], w_branch[l, i])
            merged = term if merged is None else merged + term
        out = jnp.einsum('bsd,de->bse', merged, w_out[l])
        x = x + rms_norm(out, norm_post[l]).astype(x.dtype)
    return x
```

```python
import functools
import math

import numpy as np
import jax
import jax.numpy as jnp
from jax import lax
from jax.experimental import pallas as pl
from jax.experimental.pallas import tpu as pltpu

F32, BF16, I32 = jnp.float32, jnp.bfloat16, jnp.int32
HIGHEST = lax.Precision.HIGHEST

D_MODEL = 1024
HEAD_DIM = 64
N_HEADS = 4
MIX_W = 256
RWKV_LORA = 64
RWKV_COLS = 4 * MIX_W + 2 * RWKV_LORA
RWKV_LNX_EPS = 64e-5
IDX_HEADS = 4
IDX_DIM = 64
TOPK_MAX = 256
DSA_COLS = 4 * MIX_W + IDX_HEADS * IDX_DIM + IDX_DIM + IDX_HEADS
RET_COLS = 4 * MIX_W
S5_GROUP = 16
S5_GROUPS = MIX_W // S5_GROUP
S5_STATE = 64
S5_COLS = 2 * MIX_W
XATT_COLS = 2 * MIX_W
N_BRANCH = 5
GATE_COLS = N_BRANCH * D_MODEL
ROPE_THETA = 10000.0
NORM_EPS = 1e-6

LANES = 128
VMEM_LIMIT = 48 * 1024 * 1024
INT_MIN = -(2 ** 31)

_NN = (((1,), (0,)), ((), ()))
_NT = (((1,), (1,)), ((), ()))
_TN = (((0,), (0,)), ((), ()))


def _mm(a, b, dims=_NN):
    return lax.dot_general(a.astype(BF16), b.astype(BF16), dims, preferred_element_type=F32)


def _mmf(a, b, dims=_NN):
    return lax.dot_general(a, b, dims, precision=HIGHEST, preferred_element_type=F32)


def _sigmoid(x):
    return 1.0 / (1.0 + jnp.exp(-x))


def _silu(x):
    return x * _sigmoid(x)


def _rms(x, g):
    return x * lax.rsqrt(jnp.mean(x * x, axis=-1, keepdims=True) + NORM_EPS) * g


def _head_ones(n):
    r = lax.broadcasted_iota(I32, (n, n), 0) >> 6
    c = lax.broadcasted_iota(I32, (n, n), 1) >> 6
    return jnp.where(r == c, 1.0, 0.0).astype(F32)


def _head_norm(y, bd, eps):
    mu = _mmf(y, bd) * (1.0 / HEAD_DIM)
    d = y - mu
    var = _mmf(d * d, bd) * (1.0 / HEAD_DIM)
    return d * lax.rsqrt(var + eps)


def _rope(x, cos, sin_signed):
    n = x.shape[-1]
    lane = lax.broadcasted_iota(I32, x.shape, x.ndim - 1)
    first_half = (lane & (HEAD_DIM - 1)) < HEAD_DIM // 2
    partner = jnp.where(first_half, pltpu.roll(x, n - HEAD_DIM // 2, axis=x.ndim - 1),
                        pltpu.roll(x, HEAD_DIM // 2, axis=x.ndim - 1))
    return x * cos + partner * sin_signed


def _params(*sem):
    return pltpu.CompilerParams(dimension_semantics=sem, vmem_limit_bytes=VMEM_LIMIT)


def _norm_proj_kernel(x_ref, g_ref, w_ref, *o_refs, widths):
    hb = _rms(x_ref[...], g_ref[...]).astype(BF16)
    off = 0
    for o_ref, n in zip(o_refs, widths):
        o_ref[...] = jnp.dot(hb, w_ref[:, off:off + n], preferred_element_type=F32)
        off += n


def _norm_proj(x2d, g, w_bf16, widths, tm):
    t, d = x2d.shape
    n = sum(widths)
    return pl.pallas_call(
        functools.partial(_norm_proj_kernel, widths=widths),
        grid=(t // tm,),
        in_specs=[pl.BlockSpec((tm, d), lambda i: (i, 0)),
                  pl.BlockSpec((1, d), lambda i: (0, 0)),
                  pl.BlockSpec((d, n), lambda i: (0, 0))],
        out_specs=[pl.BlockSpec((tm, w), lambda i: (i, 0)) for w in widths],
        out_shape=[jax.ShapeDtypeStruct((t, w), F32) for w in widths],
        compiler_params=_params("parallel"),
        name="norm_proj",
    )(x2d, g.reshape(1, d), w_bf16)


def _rope_table_kernel(pos_ref, inv_ref, sgn_ref, cos_ref, sin_ref):
    ang = pos_ref[0].astype(F32) * inv_ref[...]
    c = jnp.cos(ang)
    s = jnp.sin(ang) * sgn_ref[...]
    cos_ref[0] = jnp.concatenate([c, c], axis=-1)
    sin_ref[0] = jnp.concatenate([s, s], axis=-1)


def _rope_tables(positions, ts):
    b, s = positions.shape
    half = HEAD_DIM // 2
    inv = ROPE_THETA ** (-jnp.arange(half, dtype=F32) / half)
    inv = jnp.tile(inv, LANES // half).reshape(1, LANES)
    sgn = jnp.tile(jnp.concatenate([-jnp.ones((half,), F32), jnp.ones((half,), F32)]), LANES // HEAD_DIM)
    return pl.pallas_call(
        _rope_table_kernel,
        grid=(b, s // ts),
        in_specs=[pl.BlockSpec((1, ts, 1), lambda i, j: (i, j, 0)),
                  pl.BlockSpec((1, LANES), lambda i, j: (0, 0)),
                  pl.BlockSpec((1, LANES), lambda i, j: (0, 0))],
        out_specs=[pl.BlockSpec((1, ts, MIX_W), lambda i, j: (i, j, 0))] * 2,
        out_shape=[jax.ShapeDtypeStruct((b, s, MIX_W), F32)] * 2,
        compiler_params=_params("parallel", "parallel"),
        name="rope_tables",
    )(positions.reshape(b, s, 1), inv, sgn.reshape(1, LANES))


def _rwkv_kernel(c_ref, mu_ref, w0_ref, w2_ref, a0_ref, a2_ref, kk_ref, ka_ref, rk_ref, lnw_ref, lnb_ref,
                 o_ref, carry_ref, st_ref, y_ref, *, ts, ch):
    @pl.when(pl.program_id(1) == 0)
    def _():
        carry_ref[...] = jnp.zeros_like(carry_ref)
        st_ref[...] = jnp.zeros_like(st_ref)

    c = c_ref[0]
    row = lax.broadcasted_iota(I32, c.shape, 0)
    prev = jnp.where(row == 0, carry_ref[0:1, :], pltpu.roll(c, 1, axis=0))
    carry_ref[0:1, :] = c[ts - 1:ts, :]
    c = c + mu_ref[...] * (prev - c)
    w = MIX_W
    r, k, v = c[:, 0:w], c[:, w:2 * w], c[:, 2 * w:3 * w]
    wl, al = c[:, 3 * w:3 * w + RWKV_LORA], c[:, 3 * w + RWKV_LORA:3 * w + 2 * RWKV_LORA]
    g = c[:, 3 * w + 2 * RWKV_LORA:]

    z = w0_ref[...] + _mmf(jnp.tanh(wl), w2_ref[...])
    w_log = jnp.minimum(z, 0.0) - jnp.log(1.0 + jnp.exp(-jnp.abs(z))) - 0.5
    logw = -jnp.exp(w_log)
    a = _sigmoid(a0_ref[...] + _mmf(al, a2_ref[...]))
    bd = _head_ones(w)
    kk = k * kk_ref[...]
    kk = kk / jnp.maximum(jnp.sqrt(_mmf(kk * kk, bd)), 1e-12)
    kmod = k * (1.0 + (a - 1.0) * ka_ref[...])
    kka = kk * a

    shift = int(math.log2(ch))
    ri = lax.broadcasted_iota(I32, (ts, ts), 0)
    ci = lax.broadcasted_iota(I32, (ts, ts), 1)
    ltri = jnp.where((ci <= ri) & ((ri >> shift) == (ci >> shift)), 1.0, 0.0).astype(F32)
    cum = _mmf(ltri, logw)
    e_in = jnp.exp(cum)
    a_hat = -kk * jnp.exp(cum - logw)
    r_hat = r * e_in
    e_inv = jnp.exp(-cum)
    b_hat = kka * e_inv
    k_hat = kmod * e_inv

    ti = lax.broadcasted_iota(I32, (ch, ch), 0)
    si = lax.broadcasted_iota(I32, (ch, ch), 1)
    strict = si < ti
    incl = si <= ti
    eye = jnp.where(si == ti, 1.0, 0.0).astype(F32)

    for cidx in range(ts // ch):
        lo, hi = cidx * ch, (cidx + 1) * ch
        cum_end = cum[hi - 1:hi, :]
        e_tail = jnp.exp(cum_end - cum[lo:hi, :])
        b_til = kka[lo:hi, :] * e_tail
        k_til = kmod[lo:hi, :] * e_tail
        p_end = jnp.exp(cum_end)
        for h in range(N_HEADS):
            ls = slice(h * HEAD_DIM, (h + 1) * HEAD_DIM)
            ah, rh, bh, kh, vh = a_hat[lo:hi, ls], r_hat[lo:hi, ls], b_hat[lo:hi, ls], k_hat[lo:hi, ls], v[lo:hi, ls]
            st = st_ref[h]
            a_ab = jnp.where(strict, _mmf(ah, bh, _NT), 0.0)
            a_ak = jnp.where(strict, _mmf(ah, kh, _NT), 0.0)
            a_rb = jnp.where(incl, _mmf(rh, bh, _NT), 0.0)
            a_rk = jnp.where(incl, _mmf(rh, kh, _NT), 0.0)
            tinv = eye + a_ab
            pw = a_ab
            for _ in range(shift - 1):
                pw = _mmf(pw, pw)
                tinv = _mmf(tinv, eye + pw)
            u = _mmf(tinv, _mmf(ah, st, _NT) + _mmf(a_ak, vh))
            y_ref[lo:hi, ls] = _mmf(rh, st, _NT) + _mmf(a_rb, u) + _mmf(a_rk, vh)
            st_ref[h] = st * p_end[:, ls] + _mmf(u, b_til[:, ls], _TN) + _mmf(vh, k_til[:, ls], _TN)

    yn = _head_norm(y_ref[...], bd, RWKV_LNX_EPS) * lnw_ref[...] + lnb_ref[...]
    bonus = _mmf(r * kmod * rk_ref[...], bd) * v
    o_ref[0] = (yn + bonus) * _silu(g)


def _rwkv(c, mu, w0, w2, a0, a2, k_k, k_a, r_k, lnx_w, lnx_b, ts, ch):
    b, s, n = c.shape
    row = lambda t: t.reshape(1, -1)
    vec = lambda m: pl.BlockSpec((1, m), lambda i, j: (0, 0))
    mat = lambda m: pl.BlockSpec((RWKV_LORA, m), lambda i, j: (0, 0))
    return pl.pallas_call(
        functools.partial(_rwkv_kernel, ts=ts, ch=ch),
        grid=(b, s // ts),
        in_specs=[pl.BlockSpec((1, ts, n), lambda i, j: (i, j, 0)),
                  vec(n), vec(MIX_W), mat(MIX_W), vec(MIX_W), mat(MIX_W), vec(MIX_W), vec(MIX_W), vec(MIX_W),
                  vec(MIX_W), vec(MIX_W)],
        out_specs=pl.BlockSpec((1, ts, MIX_W), lambda i, j: (i, j, 0)),
        out_shape=jax.ShapeDtypeStruct((b, s, MIX_W), F32),
        scratch_shapes=[pltpu.VMEM((8, n), F32),
                        pltpu.VMEM((N_HEADS, HEAD_DIM, HEAD_DIM), F32),
                        pltpu.VMEM((ts, MIX_W), F32)],
        compiler_params=_params("parallel", "arbitrary"),
        name="rwkv7",
    )(c, row(mu), row(w0), w2, row(a0), a2, row(k_k), row(k_a), row(r_k), row(lnx_w), row(lnx_b))


def _ret_tables(tc):
    log_g = np.log(1.0 - np.exp(np.linspace(math.log(1.0 / 32), math.log(1.0 / 512), N_HEADS)))
    j = np.arange(tc, dtype=np.float64)
    rel = j[:, None] - j[None, :]
    dmask = np.where(rel >= 0, np.exp(log_g[:, None, None] * np.maximum(rel, 0.0)), 0.0)
    qdec = np.repeat(np.exp(log_g[None, :] * (j[:, None] + 1.0)), HEAD_DIM, axis=1)
    kdec = np.repeat(np.exp(log_g[None, :] * (tc - 1.0 - j[:, None])), HEAD_DIM, axis=1)
    cdec = np.repeat(np.exp(log_g * tc), HEAD_DIM)[None, :]
    f = lambda t: jnp.asarray(t, F32)
    return f(dmask), f(qdec), f(kdec), f(cdec)


def _ret_kernel(c_ref, cos_ref, sin_ref, dm_ref, qd_ref, kd_ref, cd_ref, gn_ref, o_ref, r_ref, y_ref):
    @pl.when(pl.program_id(1) == 0)
    def _():
        r_ref[...] = jnp.zeros_like(r_ref)

    w = MIX_W
    c = c_ref[0]
    cos, sin = cos_ref[0], sin_ref[0]
    q = _rope(c[:, 0:w], cos, sin)
    k = _rope(c[:, w:2 * w], cos, sin) * (HEAD_DIM ** -0.5)
    v = c[:, 2 * w:3 * w]
    g = c[:, 3 * w:]
    kd = k * kd_ref[...]
    qd = qd_ref[...]
    cd = cd_ref[...]
    for h in range(N_HEADS):
        ls = slice(h * HEAD_DIM, (h + 1) * HEAD_DIM)
        qh, kh, vh = q[:, ls], k[:, ls], v[:, ls]
        rs = r_ref[h]
        att = _mm(qh, kh, _NT) * dm_ref[h]
        y_ref[:, ls] = _mm(att, vh) + _mm(qh, rs) * qd[:, ls]
        r_ref[h] = rs * cd[:, ls] + _mm(kd[:, ls], vh, _TN)
    bd = _head_ones(w)
    o_ref[0] = _head_norm(y_ref[...], bd, NORM_EPS) * gn_ref[...] * _silu(g)


def _retention(c, cos, sin, gn_w, tc):
    b, s, n = c.shape
    dmask, qdec, kdec, cdec = _ret_tables(tc)
    tok = lambda m: pl.BlockSpec((1, tc, m), lambda i, j: (i, j, 0))
    full = lambda shape: pl.BlockSpec(shape, lambda i, j: (0,) * len(shape))
    return pl.pallas_call(
        _ret_kernel,
        grid=(b, s // tc),
        in_specs=[tok(n), tok(MIX_W), tok(MIX_W), full((N_HEADS, tc, tc)), full((tc, MIX_W)), full((tc, MIX_W)),
                  full((1, MIX_W)), full((1, MIX_W))],
        out_specs=tok(MIX_W),
        out_shape=jax.ShapeDtypeStruct((b, s, MIX_W), F32),
        scratch_shapes=[pltpu.VMEM((N_HEADS, HEAD_DIM, HEAD_DIM), F32), pltpu.VMEM((tc, MIX_W), F32)],
        compiler_params=_params("parallel", "arbitrary"),
        name="retention",
    )(c, cos, sin, dmask, qdec, kdec, cdec, gn_w.reshape(1, MIX_W))


def _s5_tables(lam_re, lam_im, log_dt, b_re, b_im, c_re, c_im, tl):
    gn, p = lam_re.shape
    lr = jnp.minimum(lam_re.astype(F32), -1e-4)
    li = lam_im.astype(F32)
    dt = jnp.exp(log_dt.astype(F32))[:, None]
    mag = jnp.exp(lr * dt)
    ab_re, ab_im = mag * jnp.cos(li * dt), mag * jnp.sin(li * dt)
    den = lr * lr + li * li
    f_re = ((ab_re - 1.0) * lr + ab_im * li) / den
    f_im = (ab_im * lr - (ab_re - 1.0) * li) / den
    bb_re = f_re[..., None] * b_re - f_im[..., None] * b_im
    bb_im = f_re[..., None] * b_im + f_im[..., None] * b_re
    eye = jnp.eye(gn, dtype=F32)
    bd_in = lambda bb: jnp.einsum('gpc,gh->gchp', bb, eye).reshape(gn * S5_GROUP, gn * p)
    w_in = jnp.concatenate([bd_in(bb_re), bd_in(bb_im)], axis=1)
    bd_out = lambda cc: jnp.einsum('gcp,gh->gphc', cc, eye).reshape(gn * p, gn * S5_GROUP)
    w_out = jnp.concatenate([bd_out(c_re), -bd_out(c_im)], axis=0)
    def powers(n):
        n = n.astype(F32)[:, None, None]
        m = jnp.exp(n * (lr * dt)[None])
        ang = n * (li * dt)[None]
        return (m * jnp.cos(ang)).reshape(-1, gn * p), (m * jnp.sin(ang)).reshape(-1, gn * p)
    steps = 2 ** jnp.arange(int(math.log2(tl)))
    dbl_re, dbl_im = powers(steps)
    car_re, car_im = powers(jnp.arange(1, tl + 1))
    return w_in, w_out, dbl_re, dbl_im, car_re, car_im


def _s5_kernel(c_ref, win_ref, wout_ref, dr_ref, di_ref, cr_ref, ci_ref, dskip_ref, wglu_ref, o_ref,
               xr_ref, xi_ref, *, tl):
    @pl.when(pl.program_id(1) == 0)
    def _():
        xr_ref[...] = jnp.zeros_like(xr_ref)
        xi_ref[...] = jnp.zeros_like(xi_ref)

    ns = S5_GROUPS * S5_STATE
    c = c_ref[0]
    u, g = c[:, :MIX_W], c[:, MIX_W:]
    bu = _mm(u, win_ref[...])
    xr, xi = bu[:, :ns], bu[:, ns:]
    row = lax.broadcasted_iota(I32, (tl, ns), 0)
    for j in range(int(math.log2(tl))):
        d = 1 << j
        ar, ai = dr_ref[j:j + 1, :], di_ref[j:j + 1, :]
        sr = jnp.where(row >= d, pltpu.roll(xr, d, axis=0), 0.0)
        si = jnp.where(row >= d, pltpu.roll(xi, d, axis=0), 0.0)
        xr, xi = xr + ar * sr - ai * si, xi + ar * si + ai * sr
    pr, pi = cr_ref[...], ci_ref[...]
    x0r, x0i = xr_ref[0:1, :], xi_ref[0:1, :]
    xr, xi = xr + pr * x0r - pi * x0i, xi + pr * x0i + pi * x0r
    xr_ref[0:1, :] = xr[tl - 1:tl, :]
    xi_ref[0:1, :] = xi[tl - 1:tl, :]
    y = _mm(jnp.concatenate([xr, xi], axis=-1), wout_ref[...]) + dskip_ref[...] * u
    y = y * (0.5 * (1.0 + jnp.tanh(math.sqrt(2.0 / math.pi) * (y + 0.044715 * (y * y * y)))))
    y = y * _sigmoid(_mm(y, wglu_ref[...]))
    o_ref[0] = y * _silu(g)


def _s5(c, lam_re, lam_im, log_dt, b_re, b_im, c_re, c_im, d_skip, w_glu, tl):
    b, s, n = c.shape
    ns = S5_GROUPS * S5_STATE
    w_in, w_out, dbl_re, dbl_im, car_re, car_im = _s5_tables(lam_re, lam_im, log_dt, b_re, b_im, c_re, c_im, tl)
    nd = dbl_re.shape[0]
    tok = lambda m: pl.BlockSpec((1, tl, m), lambda i, j: (i, j, 0))
    full = lambda shape: pl.BlockSpec(shape, lambda i, j: (0,) * len(shape))
    return pl.pallas_call(
        functools.partial(_s5_kernel, tl=tl),
        grid=(b, s // tl),
        in_specs=[tok(n), full((MIX_W, 2 * ns)), full((2 * ns, MIX_W)), full((nd, ns)), full((nd, ns)),
                  full((tl, ns)), full((tl, ns)), full((1, MIX_W)), full((MIX_W, MIX_W))],
        out_specs=tok(MIX_W),
        out_shape=jax.ShapeDtypeStruct((b, s, MIX_W), F32),
        scratch_shapes=[pltpu.VMEM((8, ns), F32), pltpu.VMEM((8, ns), F32)],
        compiler_params=_params("parallel", "arbitrary"),
        name="s5",
    )(c, w_in.astype(BF16), w_out.astype(BF16), dbl_re, dbl_im, car_re, car_im, d_skip.reshape(1, MIX_W),
      w_glu.astype(BF16))


def _xatt_kernel(c_ref, kv_ref, o_ref, y_ref):
    w = MIX_W
    c = c_ref[0]
    q, g = c[:, :w], c[:, w:]
    kv = kv_ref[0]
    for h in range(N_HEADS):
        ls = slice(h * HEAD_DIM, (h + 1) * HEAD_DIM)
        lg = _mm(q[:, ls], kv[:, ls], _NT) * (HEAD_DIM ** -0.5)
        p = jnp.exp(lg - jnp.max(lg, axis=-1, keepdims=True))
        y_ref[:, ls] = _mm(p, kv[:, w + h * HEAD_DIM:w + (h + 1) * HEAD_DIM]) / jnp.sum(p, axis=-1, keepdims=True)
    o_ref[0] = y_ref[...] * _silu(g)


def _xatt(c, mem_kv, tq):
    b, s, n = c.shape
    nm = mem_kv.shape[1]
    return pl.pallas_call(
        _xatt_kernel,
        grid=(b, s // tq),
        in_specs=[pl.BlockSpec((1, tq, n), lambda i, j: (i, j, 0)),
                  pl.BlockSpec((1, nm, 2 * MIX_W), lambda i, j: (i, 0, 0))],
        out_specs=pl.BlockSpec((1, tq, MIX_W), lambda i, j: (i, j, 0)),
        out_shape=jax.ShapeDtypeStruct((b, s, MIX_W), F32),
        scratch_shapes=[pltpu.VMEM((tq, MIX_W), F32)],
        compiler_params=_params("parallel", "parallel"),
        name="mem_xatt",
    )(c, mem_kv)


def _dsa_prep_kernel(c_ref, ikw_ref, cos_ref, sin_ref, q_ref, k_ref, v_ref, iq_ref, ik_ref):
    w = MIX_W
    c = c_ref[0]
    cos, sin = cos_ref[0], sin_ref[0]
    q = _rope(c[:, 0:w], cos, sin) * (HEAD_DIM ** -0.5)
    k = _rope(c[:, w:2 * w], cos, sin)
    v = c[:, 2 * w:3 * w]
    iq = _rope(c[:, 3 * w:4 * w], cos, sin) * (IDX_DIM ** -0.5)
    ik = _rope(ikw_ref[0], cos[:, :LANES], sin[:, :LANES])
    for h in range(N_HEADS):
        ls = slice(h * HEAD_DIM, (h + 1) * HEAD_DIM)
        q_ref[0, h] = q[:, ls].astype(BF16)
        k_ref[0, h] = k[:, ls].astype(BF16)
        v_ref[0, h] = v[:, ls].astype(BF16)
        iq_ref[0, h] = iq[:, ls].astype(BF16)
    ik_ref[0] = ik[:, :IDX_DIM].astype(BF16)


def _dsa_prep(c_main, c_ikw, cos, sin, tp):
    b, s, _ = c_main.shape
    tok = lambda m: pl.BlockSpec((1, tp, m), lambda i, j: (i, j, 0))
    hm = pl.BlockSpec((1, N_HEADS, tp, HEAD_DIM), lambda i, j: (i, 0, j, 0))
    hm_shape = jax.ShapeDtypeStruct((b, N_HEADS, s, HEAD_DIM), BF16)
    return pl.pallas_call(
        _dsa_prep_kernel,
        grid=(b, s // tp),
        in_specs=[pl.BlockSpec((1, tp, 4 * MIX_W), lambda i, j: (i, j, 0)), tok(LANES), tok(MIX_W), tok(MIX_W)],
        out_specs=[hm, hm, hm, hm, tok(IDX_DIM)],
        out_shape=[hm_shape, hm_shape, hm_shape, hm_shape, jax.ShapeDtypeStruct((b, s, IDX_DIM), BF16)],
        compiler_params=_params("parallel", "parallel"),
        name="dsa_prep",
    )(c_main, c_ikw, cos, sin)


def _dsa_kernel(q_ref, iq_ref, ikw_ref, g_ref, k_ref, v_ref, ik_ref, o_ref, key_ref, *, qb, kc, n_sel, idx_bits):
    q0 = pl.program_id(1) * qb
    nkc = (q0 + qb + kc - 1) // kc
    qidx = q0 + lax.broadcasted_iota(I32, (qb, kc), 0)
    lane = lax.broadcasted_iota(I32, (qb, kc), 1)
    lane1 = lax.broadcasted_iota(I32, (qb, LANES), 1)
    sub = kc // LANES
    iw = ikw_ref[0] * (IDX_HEADS ** -0.5)
    iws = [iw[:, IDX_DIM + h:IDX_DIM + h + 1] for h in range(IDX_HEADS)]

    def score_body(c, carry):
        ks = pl.multiple_of(c * kc, kc)
        ikc = ik_ref[0, pl.ds(ks, kc), :]
        acc = jnp.zeros((qb, kc), F32)
        for h in range(IDX_HEADS):
            s = lax.dot_general(iq_ref[0, h], ikc, _NT, preferred_element_type=F32)
            acc = acc + jnp.maximum(s, 0.0) * iws[h]
        acc = jnp.where(acc == 0.0, 0.0, acc)
        bits = pltpu.bitcast(acc, I32)
        key = jnp.where(bits < 0, bits ^ 0x7FFFFFFF, bits)
        key_ref[:, pl.ds(ks, kc)] = jnp.where(ks + lane <= qidx, key, INT_MIN)
        return carry

    lax.fori_loop(0, nkc, score_body, 0)

    def count(pred):
        def body(c, acc):
            ks = pl.multiple_of(c * kc, kc)
            for j in range(sub):
                kv = key_ref[:, pl.ds(pl.multiple_of(ks + j * LANES, LANES), LANES)]
                acc = acc + jnp.where(pred(kv, ks + j * LANES + lane1), 1.0, 0.0)
            return acc
        acc = lax.fori_loop(0, nkc, body, jnp.zeros((qb, LANES), F32))
        return jnp.broadcast_to(jnp.sum(acc, axis=1, keepdims=True), (qb, LANES))

    def bit_body(t, prefix):
        cand = prefix | jnp.left_shift(jnp.int32(1), 31 - t)
        thr = cand ^ INT_MIN
        cnt = count(lambda kv, _: kv >= thr)
        return jnp.where(cnt >= n_sel, cand, prefix)

    vstar = lax.fori_loop(0, 32, bit_body, jnp.zeros((qb, LANES), I32)) ^ INT_MIN

    need = n_sel - count(lambda kv, _: kv > vstar)

    def idx_body(t, jcut):
        cand = jcut | jnp.left_shift(jnp.int32(1), idx_bits - t)
        cnt = count(lambda kv, ki: (kv == vstar) & (ki < cand))
        return jnp.where(cnt <= need, cand, jcut)

    jcut = lax.fori_loop(0, idx_bits + 1, idx_body, jnp.zeros((qb, LANES), I32))
    vs1, jc1 = vstar[:, :1], jcut[:, :1]

    def att_body(c, carry):
        ks = pl.multiple_of(c * kc, kc)
        kv = key_ref[:, pl.ds(ks, kc)]
        sel = ((kv > vs1) | ((kv == vs1) & (ks + lane < jc1))) & (kv != INT_MIN)
        out = []
        for h in range(N_HEADS):
            m, l, acc = carry[h]
            lg = lax.dot_general(q_ref[0, h], k_ref[0, h, pl.ds(ks, kc), :], _NT, preferred_element_type=F32)
            lg = jnp.where(sel, lg, -1e30)
            m_new = jnp.maximum(m, jnp.max(lg, axis=-1, keepdims=True))
            alpha = jnp.exp(m - m_new)
            p = jnp.where(sel, jnp.exp(lg - m_new), 0.0)
            l = alpha * l + jnp.sum(p, axis=-1, keepdims=True)
            acc = alpha * acc + jnp.dot(p.astype(BF16), v_ref[0, h, pl.ds(ks, kc), :], preferred_element_type=F32)
            out.append((m_new, l, acc))
        return tuple(out)

    init = tuple((jnp.full((qb, 1), -1e30, F32), jnp.zeros((qb, 1), F32), jnp.zeros((qb, HEAD_DIM), F32))
                 for _ in range(N_HEADS))
    res = lax.fori_loop(0, nkc, att_body, init)
    y = jnp.concatenate([acc / l for (_, l, acc) in res], axis=-1)
    o_ref[0] = y * _silu(g_ref[0])


def _dsa(c_main, c_ikw, cos, sin, qb, kc, tp):
    b, s, _ = c_main.shape
    q, k, v, iq, ik = _dsa_prep(c_main, c_ikw, cos, sin, tp)
    n_sel = min(TOPK_MAX, s // 4)
    idx_bits = int(math.ceil(math.log2(s)))
    qblk = pl.BlockSpec((1, N_HEADS, qb, HEAD_DIM), lambda i, j: (i, 0, j, 0))
    kblk = pl.BlockSpec((1, N_HEADS, s, HEAD_DIM), lambda i, j: (i, 0, 0, 0))
    return pl.pallas_call(
        functools.partial(_dsa_kernel, qb=qb, kc=kc, n_sel=n_sel, idx_bits=idx_bits),
        grid=(b, s // qb),
        in_specs=[qblk, qblk,
                  pl.BlockSpec((1, qb, LANES), lambda i, j: (i, j, 0)),
                  pl.BlockSpec((1, qb, MIX_W), lambda i, j: (i, j, 4)),
                  kblk, kblk,
                  pl.BlockSpec((1, s, IDX_DIM), lambda i, j: (i, 0, 0))],
        out_specs=pl.BlockSpec((1, qb, MIX_W), lambda i, j: (i, j, 0)),
        out_shape=jax.ShapeDtypeStruct((b, s, MIX_W), F32),
        scratch_shapes=[pltpu.VMEM((qb, s), I32)],
        compiler_params=_params("parallel", "arbitrary"),
        name="dsa",
    )(q, iq, c_ikw, c_main, k, v, ik)


def _merge_kernel(x_ref, gpre_ref, gpost_ref, wg_ref, wb_ref, wo_ref, y0, y1, y2, y3, y4, o_ref):
    x = x_ref[...]
    hb = _rms(x, gpre_ref[...]).astype(BF16)
    merged = None
    for i, y_ref in enumerate((y0, y1, y2, y3, y4)):
        gate = _sigmoid(jnp.dot(hb, wg_ref[:, i * D_MODEL:(i + 1) * D_MODEL], preferred_element_type=F32))
        term = gate * jnp.dot(y_ref[...].astype(BF16), wb_ref[i], preferred_element_type=F32)
        merged = term if merged is None else merged + term
    out = jnp.dot(merged.astype(BF16), wo_ref[...], preferred_element_type=F32)
    o_ref[...] = x + _rms(out, gpost_ref[...])


def _merge(x2d, g_pre, g_post, w_gate, w_branch, w_out, ys, tm):
    t, d = x2d.shape
    tok = lambda m: pl.BlockSpec((tm, m), lambda i: (i, 0))
    full = lambda shape: pl.BlockSpec(shape, lambda i: (0,) * len(shape))
    return pl.pallas_call(
        _merge_kernel,
        grid=(t // tm,),
        in_specs=[tok(d), full((1, d)), full((1, d)), full((d, GATE_COLS)), full((N_BRANCH, MIX_W, d)),
                  full((d, d))] + [tok(MIX_W)] * N_BRANCH,
        out_specs=tok(d),
        out_shape=jax.ShapeDtypeStruct((t, d), F32),
        compiler_params=_params("parallel"),
        name="merge",
    )(x2d, g_pre.reshape(1, d), g_post.reshape(1, d), w_gate, w_branch, w_out, *ys)


def kernel(x, mem, positions, norm_pre, norm_post, norm_mem, w_in, rwkv_mu, rwkv_w0, rwkv_w2, rwkv_a0, rwkv_a2, rwkv_k_k, rwkv_k_a, rwkv_r_k, rwkv_lnx_w, rwkv_lnx_b, ret_gn_w, s5_lam_re, s5_lam_im, s5_log_dt, s5_b_re, s5_b_im, s5_c_re, s5_c_im, s5_d, s5_w_glu, w_mem_kv, w_branch, w_out):
    b, s, d = x.shape
    t = b * s
    n_mem = mem.shape[1]
    depth = w_in.shape[0]
    tm = min(256, t)
    cos, sin = _rope_tables(positions, min(512, s))
    o_dsa = RWKV_COLS
    o_ret = o_dsa + DSA_COLS
    o_s5 = o_ret + RET_COLS
    o_x = o_s5 + S5_COLS
    o_gate = o_x + XATT_COLS
    widths = (RWKV_COLS, 5 * MIX_W, LANES, RET_COLS, S5_COLS, XATT_COLS)
    for l in range(depth):
        w = w_in[l]
        o_idx = o_dsa + 4 * MIX_W
        o_g = o_idx + IDX_DIM + IDX_HEADS
        w_cat = jnp.concatenate([
            w[:, :RWKV_COLS],
            w[:, o_dsa:o_idx], w[:, o_g:o_ret],
            w[:, o_idx:o_g], jnp.zeros((d, LANES - IDX_DIM - IDX_HEADS), w.dtype),
            w[:, o_ret:o_gate]], axis=1).astype(BF16)
        x2 = x.reshape(t, d)
        c_rwkv, c_dsa, c_ikw, c_ret, c_s5, c_x = _norm_proj(x2, norm_pre[l], w_cat, widths, tm)
        (mem_kv,) = _norm_proj(mem.reshape(b * n_mem, d), norm_mem[l], w_mem_kv[l].astype(BF16), (2 * MIX_W,),
                               min(256, b * n_mem))
        r3 = lambda a: a.reshape(b, s, a.shape[-1])
        y_rwkv = _rwkv(r3(c_rwkv), rwkv_mu[l], rwkv_w0[l], rwkv_w2[l], rwkv_a0[l], rwkv_a2[l], rwkv_k_k[l],
                       rwkv_k_a[l], rwkv_r_k[l], rwkv_lnx_w[l], rwkv_lnx_b[l], ts=min(256, s), ch=64)
        y_dsa = _dsa(r3(c_dsa), r3(c_ikw), cos, sin, qb=128, kc=min(512, s), tp=min(512, s))
        y_ret = _retention(r3(c_ret), cos, sin, ret_gn_w[l], tc=min(256, s))
        y_s5 = _s5(r3(c_s5), s5_lam_re[l], s5_lam_im[l], s5_log_dt[l], s5_b_re[l], s5_b_im[l], s5_c_re[l],
                   s5_c_im[l], s5_d[l], s5_w_glu[l], tl=min(128, s))
        y_x = _xatt(r3(c_x), mem_kv.reshape(b, n_mem, 2 * MIX_W), tq=min(512, s))
        ys = [y.reshape(t, MIX_W) for y in (y_rwkv, y_dsa, y_ret, y_s5, y_x)]
        x = _merge(x2, norm_pre[l], norm_post[l], w[:, o_gate:].astype(BF16), w_branch[l].astype(BF16),
                   w_out[l].astype(BF16), ys, tm).reshape(b, s, d)
    return x
```

```python
import functools
import math

import numpy as np
import jax
import jax.numpy as jnp
from jax import lax
from jax.experimental import pallas as pl
from jax.experimental.pallas import tpu as pltpu

F32, BF16, I32 = jnp.float32, jnp.bfloat16, jnp.int32
HIGHEST = lax.Precision.HIGHEST

D_MODEL = 1024
HEAD_DIM = 64
N_HEADS = 4
MIX_W = 256
RWKV_LORA = 64
RWKV_COLS = 4 * MIX_W + 2 * RWKV_LORA
RWKV_LNX_EPS = 64e-5
IDX_HEADS = 4
IDX_DIM = 64
TOPK_MAX = 256
DSA_COLS = 4 * MIX_W + IDX_HEADS * IDX_DIM + IDX_DIM + IDX_HEADS
RET_COLS = 4 * MIX_W
S5_GROUP = 16
S5_GROUPS = MIX_W // S5_GROUP
S5_STATE = 64
S5_COLS = 2 * MIX_W
XATT_COLS = 2 * MIX_W
N_BRANCH = 5
GATE_COLS = N_BRANCH * D_MODEL
ROPE_THETA = 10000.0
NORM_EPS = 1e-6

LANES = 128
VMEM_LIMIT = 48 * 1024 * 1024
INT_MIN = -(2 ** 31)

_NN = (((1,), (0,)), ((), ()))
_NT = (((1,), (1,)), ((), ()))
_TN = (((0,), (0,)), ((), ()))


def _mm(a, b, dims=_NN):
    return lax.dot_general(a.astype(BF16), b.astype(BF16), dims, preferred_element_type=F32)


def _split(a):
    hi = a.astype(BF16)
    return hi, (a - hi.astype(F32)).astype(BF16)


def _mm3s(a, b, dims=_NN):
    d = lambda x, y: lax.dot_general(x, y, dims, preferred_element_type=F32)
    return d(a[0], b[0]) + (d(a[0], b[1]) + d(a[1], b[0]))


def _mm3(a, b, dims=_NN):
    return _mm3s(_split(a), _split(b), dims)


def _split3(x):
    hi = x.astype(BF16)
    r1 = x - hi.astype(F32)
    mid = r1.astype(BF16)
    return hi, mid, (r1 - mid.astype(F32)).astype(BF16)


def _mm_ones_rhs(x, ones_bf16):
    d = lambda p: jnp.dot(p, ones_bf16, preferred_element_type=F32)
    hi, mid, lo = _split3(x)
    return d(hi) + (d(mid) + d(lo))


def _mm_ones_lhs(ones_bf16, x):
    d = lambda p: jnp.dot(ones_bf16, p, preferred_element_type=F32)
    hi, mid, lo = _split3(x)
    return d(hi) + (d(mid) + d(lo))


def _sigmoid(x):
    return 1.0 / (1.0 + jnp.exp(-x))


def _silu(x):
    return x * _sigmoid(x)


def _rms(x, g):
    return x * lax.rsqrt(jnp.mean(x * x, axis=-1, keepdims=True) + NORM_EPS) * g


def _head_ones(n):
    r = lax.broadcasted_iota(I32, (n, n), 0) >> 6
    c = lax.broadcasted_iota(I32, (n, n), 1) >> 6
    return jnp.where(r == c, 1.0, 0.0).astype(BF16)


def _head_norm(y, bd, eps):
    mu = _mm_ones_rhs(y, bd) * (1.0 / HEAD_DIM)
    d = y - mu
    var = _mm_ones_rhs(d * d, bd) * (1.0 / HEAD_DIM)
    return d * lax.rsqrt(var + eps)


def _bit_transpose32(a):
    a = list(a)
    j, m = 16, 0x0000FFFF
    while j:
        k = 0
        while k < 32:
            t = (a[k] ^ lax.shift_right_logical(a[k + j], jnp.full_like(a[k + j], j))) & m
            a[k] = a[k] ^ t
            a[k + j] = a[k + j] ^ (t << j)
            k = (k + j + 1) & ~j
        j >>= 1
        m = m ^ (m << j)
    return a


def _rope(x, cos, sin_signed):
    n = x.shape[-1]
    lane = lax.broadcasted_iota(I32, x.shape, x.ndim - 1)
    first_half = (lane & (HEAD_DIM - 1)) < HEAD_DIM // 2
    partner = jnp.where(first_half, pltpu.roll(x, n - HEAD_DIM // 2, axis=x.ndim - 1),
                        pltpu.roll(x, HEAD_DIM // 2, axis=x.ndim - 1))
    return x * cos + partner * sin_signed


def _params(*sem):
    return pltpu.CompilerParams(dimension_semantics=sem, vmem_limit_bytes=VMEM_LIMIT)


def _norm_proj_kernel(x_ref, g_ref, w_ref, *o_refs, widths):
    hb = _rms(x_ref[...], g_ref[...]).astype(BF16)
    off = 0
    for o_ref, n in zip(o_refs, widths):
        o_ref[...] = jnp.dot(hb, w_ref[:, off:off + n], preferred_element_type=F32)
        off += n


def _norm_proj(x2d, g, w_bf16, widths, tm):
    t, d = x2d.shape
    n = sum(widths)
    return pl.pallas_call(
        functools.partial(_norm_proj_kernel, widths=widths),
        grid=(t // tm,),
        in_specs=[pl.BlockSpec((tm, d), lambda i: (i, 0)),
                  pl.BlockSpec((1, d), lambda i: (0, 0)),
                  pl.BlockSpec((d, n), lambda i: (0, 0))],
        out_specs=[pl.BlockSpec((tm, w), lambda i: (i, 0)) for w in widths],
        out_shape=[jax.ShapeDtypeStruct((t, w), F32) for w in widths],
        compiler_params=_params("parallel"),
        name="norm_proj",
    )(x2d, g.reshape(1, d), w_bf16)


def _cast_kernel(x_ref, o_ref):
    o_ref[...] = x_ref[...].astype(o_ref.dtype)


def _to_bf16(w):
    w2 = w.reshape(-1, w.shape[-1])
    r, n = w2.shape
    tr = min(256, r)
    out = pl.pallas_call(
        _cast_kernel,
        grid=(r // tr,),
        in_specs=[pl.BlockSpec((tr, n), lambda i: (i, 0))],
        out_specs=pl.BlockSpec((tr, n), lambda i: (i, 0)),
        out_shape=jax.ShapeDtypeStruct((r, n), BF16),
        compiler_params=_params("parallel"),
        name="to_bf16",
    )(w2)
    return out.reshape(w.shape)


def _rope_table_kernel(pos_ref, inv_ref, sgn_ref, cos_ref, sin_ref):
    ang = pos_ref[0].astype(F32) * inv_ref[...]
    c = jnp.cos(ang)
    s = jnp.sin(ang) * sgn_ref[...]
    cos_ref[0] = jnp.concatenate([c, c], axis=-1)
    sin_ref[0] = jnp.concatenate([s, s], axis=-1)


def _rope_tables(positions, ts):
    b, s = positions.shape
    half = HEAD_DIM // 2
    inv = ROPE_THETA ** (-jnp.arange(half, dtype=F32) / half)
    inv = jnp.tile(inv, LANES // half).reshape(1, LANES)
    sgn = jnp.tile(jnp.concatenate([-jnp.ones((half,), F32), jnp.ones((half,), F32)]), LANES // HEAD_DIM)
    return pl.pallas_call(
        _rope_table_kernel,
        grid=(b, s // ts),
        in_specs=[pl.BlockSpec((1, ts, 1), lambda i, j: (i, j, 0)),
                  pl.BlockSpec((1, LANES), lambda i, j: (0, 0)),
                  pl.BlockSpec((1, LANES), lambda i, j: (0, 0))],
        out_specs=[pl.BlockSpec((1, ts, MIX_W), lambda i, j: (i, j, 0))] * 2,
        out_shape=[jax.ShapeDtypeStruct((b, s, MIX_W), F32)] * 2,
        compiler_params=_params("parallel", "parallel"),
        name="rope_tables",
    )(positions.reshape(b, s, 1), inv, sgn.reshape(1, LANES))


def _rwkv_kernel(c_ref, mu_ref, w0_ref, w2_ref, a0_ref, a2_ref, kk_ref, ka_ref, rk_ref, lnw_ref, lnb_ref,
                 o_ref, carry_ref, st_ref, y_ref, *, ts, ch):
    @pl.when(pl.program_id(1) == 0)
    def _():
        carry_ref[...] = jnp.zeros_like(carry_ref)
        st_ref[...] = jnp.zeros_like(st_ref)

    c = c_ref[0]
    row = lax.broadcasted_iota(I32, c.shape, 0)
    prev = jnp.where(row == 0, carry_ref[0:1, :], pltpu.roll(c, 1, axis=0))
    carry_ref[0:1, :] = c[ts - 1:ts, :]
    c = c + mu_ref[...] * (prev - c)
    w = MIX_W
    r, k, v = c[:, 0:w], c[:, w:2 * w], c[:, 2 * w:3 * w]
    wl, al = c[:, 3 * w:3 * w + RWKV_LORA], c[:, 3 * w + RWKV_LORA:3 * w + 2 * RWKV_LORA]
    g = c[:, 3 * w + 2 * RWKV_LORA:]

    z = w0_ref[...] + _mm3(jnp.tanh(wl), w2_ref[...])
    w_log = jnp.minimum(z, 0.0) - jnp.log(1.0 + jnp.exp(-jnp.abs(z))) - 0.5
    logw = -jnp.exp(w_log)
    a = _sigmoid(a0_ref[...] + _mm3(al, a2_ref[...]))
    bd = _head_ones(w)
    kk = k * kk_ref[...]
    kk = kk / jnp.maximum(jnp.sqrt(_mm_ones_rhs(kk * kk, bd)), 1e-12)
    kmod = k * (1.0 + (a - 1.0) * ka_ref[...])
    kka = kk * a

    shift = int(math.log2(ch))
    ri = lax.broadcasted_iota(I32, (ts, ts), 0)
    ci = lax.broadcasted_iota(I32, (ts, ts), 1)
    ltri = jnp.where((ci <= ri) & ((ri >> shift) == (ci >> shift)), 1.0, 0.0).astype(BF16)
    cum = _mm_ones_lhs(ltri, logw)
    e_in = jnp.exp(cum)
    a_hat = -kk * jnp.exp(cum - logw)
    r_hat = r * e_in
    e_inv = jnp.exp(-cum)
    b_hat = kka * e_inv
    k_hat = kmod * e_inv

    ti = lax.broadcasted_iota(I32, (2 * ch, ch), 0)
    si = lax.broadcasted_iota(I32, (2 * ch, ch), 1)
    tri2 = si < jnp.where(ti < ch, ti, ti - ch + 1)
    eye = jnp.where(lax.broadcasted_iota(I32, (ch, ch), 0) == lax.broadcasted_iota(I32, (ch, ch), 1), 1.0, 0.0)

    for cidx in range(ts // ch):
        lo, hi = cidx * ch, (cidx + 1) * ch
        cum_end = cum[hi - 1:hi, :]
        e_tail = jnp.exp(cum_end - cum[lo:hi, :])
        b_til = kka[lo:hi, :] * e_tail
        k_til = kmod[lo:hi, :] * e_tail
        p_end = jnp.exp(cum_end)
        for h in range(N_HEADS):
            ls = slice(h * HEAD_DIM, (h + 1) * HEAD_DIM)
            vh = v[lo:hi, ls]
            ar = _split(jnp.concatenate([a_hat[lo:hi, ls], r_hat[lo:hi, ls]], axis=0))
            st = st_ref[h]
            x_b = jnp.where(tri2, _mm3s(ar, _split(b_hat[lo:hi, ls]), _NT), 0.0)
            x_k = jnp.where(tri2, _mm3s(ar, _split(k_hat[lo:hi, ls]), _NT), 0.0)
            a_ab = x_b[:ch]
            tinv = eye + a_ab
            pw = a_ab
            for _ in range(shift - 1):
                pws = _split(pw)
                pw = _mm3s(pws, pws)
                tinv = _mm3(tinv, eye + pw)
            from_state = _mm3s(ar, _split(st), _NT)
            from_v = _mm3(x_k, vh)
            u = _mm3(tinv, from_state[:ch] + from_v[:ch])
            y_ref[lo:hi, ls] = from_state[ch:] + from_v[ch:] + _mm3(x_b[ch:], u)
            st_ref[h] = st * p_end[:, ls] + _mm3(jnp.concatenate([u, vh], axis=0),
                                                 jnp.concatenate([b_til[:, ls], k_til[:, ls]], axis=0), _TN)

    yn = _head_norm(y_ref[...], bd, RWKV_LNX_EPS) * lnw_ref[...] + lnb_ref[...]
    bonus = _mm_ones_rhs(r * kmod * rk_ref[...], bd) * v
    o_ref[0] = (yn + bonus) * _silu(g)


def _rwkv(c, mu, w0, w2, a0, a2, k_k, k_a, r_k, lnx_w, lnx_b, ts, ch):
    b, s, n = c.shape
    row = lambda t: t.reshape(1, -1)
    vec = lambda m: pl.BlockSpec((1, m), lambda i, j: (0, 0))
    mat = lambda m: pl.BlockSpec((RWKV_LORA, m), lambda i, j: (0, 0))
    return pl.pallas_call(
        functools.partial(_rwkv_kernel, ts=ts, ch=ch),
        grid=(b, s // ts),
        in_specs=[pl.BlockSpec((1, ts, n), lambda i, j: (i, j, 0)),
                  vec(n), vec(MIX_W), mat(MIX_W), vec(MIX_W), mat(MIX_W), vec(MIX_W), vec(MIX_W), vec(MIX_W),
                  vec(MIX_W), vec(MIX_W)],
        out_specs=pl.BlockSpec((1, ts, MIX_W), lambda i, j: (i, j, 0)),
        out_shape=jax.ShapeDtypeStruct((b, s, MIX_W), F32),
        scratch_shapes=[pltpu.VMEM((8, n), F32),
                        pltpu.VMEM((N_HEADS, HEAD_DIM, HEAD_DIM), F32),
                        pltpu.VMEM((ts, MIX_W), F32)],
        compiler_params=_params("parallel", "arbitrary"),
        name="rwkv7",
    )(c, row(mu), row(w0), w2, row(a0), a2, row(k_k), row(k_a), row(r_k), row(lnx_w), row(lnx_b))


def _ret_tables(tc):
    log_g = np.log(1.0 - np.exp(np.linspace(math.log(1.0 / 32), math.log(1.0 / 512), N_HEADS)))
    j = np.arange(tc, dtype=np.float64)
    rel = j[:, None] - j[None, :]
    dmask = np.where(rel >= 0, np.exp(log_g[:, None, None] * np.maximum(rel, 0.0)), 0.0)
    qdec = np.repeat(np.exp(log_g[None, :] * (j[:, None] + 1.0)), HEAD_DIM, axis=1)
    kdec = np.repeat(np.exp(log_g[None, :] * (tc - 1.0 - j[:, None])), HEAD_DIM, axis=1)
    cdec = np.repeat(np.exp(log_g * tc), HEAD_DIM)[None, :]
    f = lambda t: jnp.asarray(t, F32)
    return f(dmask), f(qdec), f(kdec), f(cdec)


def _ret_kernel(c_ref, cos_ref, sin_ref, dm_ref, qd_ref, kd_ref, cd_ref, gn_ref, o_ref, r_ref, y_ref):
    @pl.when(pl.program_id(1) == 0)
    def _():
        r_ref[...] = jnp.zeros_like(r_ref)

    w = MIX_W
    c = c_ref[0]
    cos, sin = cos_ref[0], sin_ref[0]
    q = _rope(c[:, 0:w], cos, sin)
    k = _rope(c[:, w:2 * w], cos, sin) * (HEAD_DIM ** -0.5)
    v = c[:, 2 * w:3 * w]
    g = c[:, 3 * w:]
    kd = k * kd_ref[...]
    qd = qd_ref[...]
    cd = cd_ref[...]
    for h in range(N_HEADS):
        ls = slice(h * HEAD_DIM, (h + 1) * HEAD_DIM)
        qh, kh, vh = q[:, ls], k[:, ls], v[:, ls]
        rs = r_ref[h]
        att = _mm(qh, kh, _NT) * dm_ref[h]
        y_ref[:, ls] = _mm(att, vh) + _mm(qh, rs) * qd[:, ls]
        r_ref[h] = rs * cd[:, ls] + _mm(kd[:, ls], vh, _TN)
    bd = _head_ones(w)
    o_ref[0] = _head_norm(y_ref[...], bd, NORM_EPS) * gn_ref[...] * _silu(g)


def _retention(c, cos, sin, gn_w, tc):
    b, s, n = c.shape
    dmask, qdec, kdec, cdec = _ret_tables(tc)
    tok = lambda m: pl.BlockSpec((1, tc, m), lambda i, j: (i, j, 0))
    full = lambda shape: pl.BlockSpec(shape, lambda i, j: (0,) * len(shape))
    return pl.pallas_call(
        _ret_kernel,
        grid=(b, s // tc),
        in_specs=[tok(n), tok(MIX_W), tok(MIX_W), full((N_HEADS, tc, tc)), full((tc, MIX_W)), full((tc, MIX_W)),
                  full((1, MIX_W)), full((1, MIX_W))],
        out_specs=tok(MIX_W),
        out_shape=jax.ShapeDtypeStruct((b, s, MIX_W), F32),
        scratch_shapes=[pltpu.VMEM((N_HEADS, HEAD_DIM, HEAD_DIM), F32), pltpu.VMEM((tc, MIX_W), F32)],
        compiler_params=_params("parallel", "arbitrary"),
        name="retention",
    )(c, cos, sin, dmask, qdec, kdec, cdec, gn_w.reshape(1, MIX_W))


def _s5_tables(lam_re, lam_im, log_dt, b_re, b_im, c_re, c_im, tl):
    gn, p = lam_re.shape
    lr = jnp.minimum(lam_re.astype(F32), -1e-4)
    li = lam_im.astype(F32)
    dt = jnp.exp(log_dt.astype(F32))[:, None]
    mag = jnp.exp(lr * dt)
    ab_re, ab_im = mag * jnp.cos(li * dt), mag * jnp.sin(li * dt)
    den = lr * lr + li * li
    f_re = ((ab_re - 1.0) * lr + ab_im * li) / den
    f_im = (ab_im * lr - (ab_re - 1.0) * li) / den
    bb_re = f_re[..., None] * b_re - f_im[..., None] * b_im
    bb_im = f_re[..., None] * b_im + f_im[..., None] * b_re
    eye = jnp.eye(gn, dtype=F32)
    bd_in = lambda bb: jnp.einsum('gpc,gh->gchp', bb, eye).reshape(gn * S5_GROUP, gn * p)
    w_in = jnp.concatenate([bd_in(bb_re), bd_in(bb_im)], axis=1)
    bd_out = lambda cc: jnp.einsum('gcp,gh->gphc', cc, eye).reshape(gn * p, gn * S5_GROUP)
    w_out = jnp.concatenate([bd_out(c_re), -bd_out(c_im)], axis=0)
    def powers(n):
        n = n.astype(F32)[:, None, None]
        m = jnp.exp(n * (lr * dt)[None])
        ang = n * (li * dt)[None]
        return (m * jnp.cos(ang)).reshape(-1, gn * p), (m * jnp.sin(ang)).reshape(-1, gn * p)
    steps = 2 ** jnp.arange(int(math.log2(tl)))
    dbl_re, dbl_im = powers(steps)
    car_re, car_im = powers(jnp.arange(1, tl + 1))
    return w_in, w_out, dbl_re, dbl_im, car_re, car_im


def _s5_kernel(c_ref, win_ref, wout_ref, dr_ref, di_ref, cr_ref, ci_ref, dskip_ref, wglu_ref, o_ref,
               xr_ref, xi_ref, *, tl):
    @pl.when(pl.program_id(1) == 0)
    def _():
        xr_ref[...] = jnp.zeros_like(xr_ref)
        xi_ref[...] = jnp.zeros_like(xi_ref)

    ns = S5_GROUPS * S5_STATE
    c = c_ref[0]
    u, g = c[:, :MIX_W], c[:, MIX_W:]
    bu = _mm(u, win_ref[...])
    xr, xi = bu[:, :ns], bu[:, ns:]
    row = lax.broadcasted_iota(I32, (tl, ns), 0)
    for j in range(int(math.log2(tl))):
        d = 1 << j
        ar, ai = dr_ref[j:j + 1, :], di_ref[j:j + 1, :]
        sr = jnp.where(row >= d, pltpu.roll(xr, d, axis=0), 0.0)
        si = jnp.where(row >= d, pltpu.roll(xi, d, axis=0), 0.0)
        xr, xi = xr + ar * sr - ai * si, xi + ar * si + ai * sr
    pr, pi = cr_ref[...], ci_ref[...]
    x0r, x0i = xr_ref[0:1, :], xi_ref[0:1, :]
    xr, xi = xr + pr * x0r - pi * x0i, xi + pr * x0i + pi * x0r
    xr_ref[0:1, :] = xr[tl - 1:tl, :]
    xi_ref[0:1, :] = xi[tl - 1:tl, :]
    y = _mm(jnp.concatenate([xr, xi], axis=-1), wout_ref[...]) + dskip_ref[...] * u
    y = y * (0.5 * (1.0 + jnp.tanh(math.sqrt(2.0 / math.pi) * (y + 0.044715 * (y * y * y)))))
    y = y * _sigmoid(_mm(y, wglu_ref[...]))
    o_ref[0] = y * _silu(g)


def _s5(c, lam_re, lam_im, log_dt, b_re, b_im, c_re, c_im, d_skip, w_glu, tl):
    b, s, n = c.shape
    ns = S5_GROUPS * S5_STATE
    w_in, w_out, dbl_re, dbl_im, car_re, car_im = _s5_tables(lam_re, lam_im, log_dt, b_re, b_im, c_re, c_im, tl)
    nd = dbl_re.shape[0]
    tok = lambda m: pl.BlockSpec((1, tl, m), lambda i, j: (i, j, 0))
    full = lambda shape: pl.BlockSpec(shape, lambda i, j: (0,) * len(shape))
    return pl.pallas_call(
        functools.partial(_s5_kernel, tl=tl),
        grid=(b, s // tl),
        in_specs=[tok(n), full((MIX_W, 2 * ns)), full((2 * ns, MIX_W)), full((nd, ns)), full((nd, ns)),
                  full((tl, ns)), full((tl, ns)), full((1, MIX_W)), full((MIX_W, MIX_W))],
        out_specs=tok(MIX_W),
        out_shape=jax.ShapeDtypeStruct((b, s, MIX_W), F32),
        scratch_shapes=[pltpu.VMEM((8, ns), F32), pltpu.VMEM((8, ns), F32)],
        compiler_params=_params("parallel", "arbitrary"),
        name="s5",
    )(c, w_in.astype(BF16), w_out.astype(BF16), dbl_re, dbl_im, car_re, car_im, d_skip.reshape(1, MIX_W),
      w_glu.astype(BF16))


def _xatt_kernel(c_ref, kv_ref, o_ref, y_ref):
    w = MIX_W
    c = c_ref[0]
    q, g = c[:, :w], c[:, w:]
    kv = kv_ref[0]
    for h in range(N_HEADS):
        ls = slice(h * HEAD_DIM, (h + 1) * HEAD_DIM)
        lg = _mm(q[:, ls], kv[:, ls], _NT) * (HEAD_DIM ** -0.5)
        p = jnp.exp(lg - jnp.max(lg, axis=-1, keepdims=True))
        y_ref[:, ls] = _mm(p, kv[:, w + h * HEAD_DIM:w + (h + 1) * HEAD_DIM]) / jnp.sum(p, axis=-1, keepdims=True)
    o_ref[0] = y_ref[...] * _silu(g)


def _xatt(c, mem_kv, tq):
    b, s, n = c.shape
    nm = mem_kv.shape[1]
    return pl.pallas_call(
        _xatt_kernel,
        grid=(b, s // tq),
        in_specs=[pl.BlockSpec((1, tq, n), lambda i, j: (i, j, 0)),
                  pl.BlockSpec((1, nm, 2 * MIX_W), lambda i, j: (i, 0, 0))],
        out_specs=pl.BlockSpec((1, tq, MIX_W), lambda i, j: (i, j, 0)),
        out_shape=jax.ShapeDtypeStruct((b, s, MIX_W), F32),
        scratch_shapes=[pltpu.VMEM((tq, MIX_W), F32)],
        compiler_params=_params("parallel", "parallel"),
        name="mem_xatt",
    )(c, mem_kv)


def _dsa_prep_kernel(c_ref, ikw_ref, cos_ref, sin_ref, q_ref, k_ref, v_ref, iq_ref, ik_ref):
    w = MIX_W
    c = c_ref[0]
    cos, sin = cos_ref[0], sin_ref[0]
    q = _rope(c[:, 0:w], cos, sin) * (HEAD_DIM ** -0.5)
    k = _rope(c[:, w:2 * w], cos, sin)
    v = c[:, 2 * w:3 * w]
    iq = _rope(c[:, 3 * w:4 * w], cos, sin) * (IDX_DIM ** -0.5)
    ik = _rope(ikw_ref[0], cos[:, :LANES], sin[:, :LANES])
    one_col = jnp.where(lax.broadcasted_iota(I32, (c.shape[0], HEAD_DIM), 1) == 0, 1.0, 0.0)
    for h in range(N_HEADS):
        ls = slice(h * HEAD_DIM, (h + 1) * HEAD_DIM)
        q_ref[0, h] = q[:, ls].astype(BF16)
        k_ref[0, h] = k[:, ls].astype(BF16)
        v_ref[0, h] = jnp.concatenate([v[:, ls], one_col], axis=-1).astype(BF16)
        iq_ref[0, h] = iq[:, ls].astype(BF16)
    ik_ref[0] = ik[:, :IDX_DIM].astype(BF16)


def _dsa_prep(c_main, c_ikw, cos, sin, tp):
    b, s, _ = c_main.shape
    tok = lambda m: pl.BlockSpec((1, tp, m), lambda i, j: (i, j, 0))
    hm = lambda m: pl.BlockSpec((1, N_HEADS, tp, m), lambda i, j: (i, 0, j, 0))
    hm_shape = lambda m: jax.ShapeDtypeStruct((b, N_HEADS, s, m), BF16)
    return pl.pallas_call(
        _dsa_prep_kernel,
        grid=(b, s // tp),
        in_specs=[pl.BlockSpec((1, tp, 4 * MIX_W), lambda i, j: (i, j, 0)), tok(LANES), tok(MIX_W), tok(MIX_W)],
        out_specs=[hm(HEAD_DIM), hm(HEAD_DIM), hm(2 * HEAD_DIM), hm(HEAD_DIM), tok(IDX_DIM)],
        out_shape=[hm_shape(HEAD_DIM), hm_shape(HEAD_DIM), hm_shape(2 * HEAD_DIM), hm_shape(HEAD_DIM),
                   jax.ShapeDtypeStruct((b, s, IDX_DIM), BF16)],
        compiler_params=_params("parallel", "parallel"),
        name="dsa_prep",
    )(c_main, c_ikw, cos, sin)


def _dsa_kernel(q_ref, iq_ref, ikw_ref, g_ref, k_ref, v_ref, ik_ref, tri_ref, o_ref, key_ref, planes_ref, m_ref,
                acc_ref, *, qb, kc, n_sel, n_tile):
    q0 = pl.program_id(1) * qb
    nkc = (q0 + qb + kc - 1) // kc
    qidx = q0 + lax.broadcasted_iota(I32, (qb, kc), 0)
    lane = lax.broadcasted_iota(I32, (qb, kc), 1)
    wide = lambda t: jnp.concatenate([t] * (kc // LANES), axis=1)
    ones_k = jnp.ones((kc, LANES), BF16)
    chunk = lambda c: pl.ds(pl.multiple_of(c * kc, kc), kc)
    iw = ikw_ref[0] * (IDX_HEADS ** -0.5)
    iws = [iw[:, IDX_DIM + h:IDX_DIM + h + 1] for h in range(IDX_HEADS)]

    def score_body(c, carry):
        ikc = ik_ref[0, chunk(c), :]
        acc = jnp.zeros((qb, kc), F32)
        for h in range(IDX_HEADS):
            s = lax.dot_general(iq_ref[0, h], ikc, _NT, preferred_element_type=F32)
            acc = acc + jnp.maximum(s, 0.0) * iws[h]
        acc = jnp.where(acc == 0.0, 0.0, acc)
        bits = pltpu.bitcast(acc, I32)
        key = jnp.where(bits < 0, bits ^ 0x7FFFFFFF, bits)
        key_ref[:, chunk(c)] = jnp.where(c * kc + lane <= qidx, key, INT_MIN)
        return carry

    lax.fori_loop(0, nkc, score_body, 0)

    def fill_body(c, carry):
        key_ref[:, chunk(c)] = jnp.full((qb, kc), INT_MIN, I32)
        return carry

    lax.fori_loop(nkc, n_tile * LANES // kc, fill_body, 0)

    def plane_body(gidx, carry):
        rows = pl.ds(pl.multiple_of(gidx * 8, 8), 8)
        tiles = [key_ref[rows, i * LANES:(i + 1) * LANES] ^ INT_MIN for i in range(n_tile)]
        tiles += [jnp.zeros((8, LANES), I32)] * (32 - n_tile)
        for p, plane in enumerate(_bit_transpose32(tiles)):
            planes_ref[p, rows, :] = plane
        return carry

    lax.fori_loop(0, qb // 8, plane_body, 0)

    ones_l = jnp.ones((LANES, LANES), BF16)

    def bit_body(p, carry):
        alive, above, vbits = carry
        ones = alive & planes_ref[p]
        c1 = jnp.dot(lax.population_count(ones).astype(F32).astype(BF16), ones_l, preferred_element_type=F32)
        take = above + c1 >= n_sel
        return (jnp.where(take, ones, alive ^ ones), jnp.where(take, above, above + c1),
                jnp.where(take, vbits | jnp.left_shift(jnp.int32(1), 31 - p), vbits))

    _, above, vbits = lax.fori_loop(
        0, 32, bit_body, (jnp.full((qb, LANES), -1, I32), jnp.zeros((qb, LANES), F32), jnp.zeros((qb, LANES), I32)))
    vs = wide(jnp.maximum(vbits ^ INT_MIN, INT_MIN + 1))
    need = wide(n_sel - above)

    def bias_body(c, run):
        kv = key_ref[:, chunk(c)]
        tied = kv == vs
        tied_b = jnp.where(tied, 1.0, 0.0).astype(BF16)
        rank = wide(run) + jnp.dot(tied_b, tri_ref[...], preferred_element_type=F32)
        bias = jnp.where(kv > vs, 0.0, jnp.where(tied, jnp.where(rank <= need, 0.0, -1e30), -1e30))
        key_ref[:, chunk(c)] = pltpu.bitcast(bias, I32)
        return run + jnp.dot(tied_b, ones_k, preferred_element_type=F32)

    lax.fori_loop(0, nkc, bias_body, jnp.zeros((qb, LANES), F32))

    m_ref[...] = jnp.full(m_ref.shape, -1e30, F32)
    acc_ref[...] = jnp.zeros(acc_ref.shape, F32)

    def att_body(c, carry):
        bias = pltpu.bitcast(key_ref[:, chunk(c)], F32)
        for h in range(N_HEADS):
            lg = lax.dot_general(q_ref[0, h], k_ref[0, h, chunk(c), :], _NT, preferred_element_type=F32) + bias
            m_old = m_ref[h]
            m_new = jnp.maximum(m_old, jnp.broadcast_to(jnp.max(lg, axis=-1, keepdims=True), (qb, LANES)))
            p = jnp.exp(lg - wide(m_new)).astype(BF16)
            acc_ref[h] = jnp.exp(m_old - m_new) * acc_ref[h] + jnp.dot(p, v_ref[0, h, chunk(c), :],
                                                                       preferred_element_type=F32)
            m_ref[h] = m_new
        return carry

    lax.fori_loop(0, nkc, att_body, 0)
    y = jnp.concatenate([acc_ref[h][:, :HEAD_DIM] / acc_ref[h][:, HEAD_DIM:HEAD_DIM + 1] for h in range(N_HEADS)],
                        axis=-1)
    o_ref[0] = y * _silu(g_ref[0])


def _dsa(c_main, c_ikw, cos, sin, qb, kc, tp):
    b, s, _ = c_main.shape
    q, k, v, iq, ik = _dsa_prep(c_main, c_ikw, cos, sin, tp)
    n_sel = min(TOPK_MAX, s // 4)
    n_tile = s // LANES
    assert n_tile <= 32 and s % kc == 0, "one key per bit of a 32-bit plane word"
    tri = jnp.asarray(np.triu(np.ones((kc, kc), np.float32)), BF16)
    qblk = pl.BlockSpec((1, N_HEADS, qb, HEAD_DIM), lambda i, j: (i, 0, j, 0))
    return pl.pallas_call(
        functools.partial(_dsa_kernel, qb=qb, kc=kc, n_sel=n_sel, n_tile=n_tile),
        grid=(b, s // qb),
        in_specs=[qblk, qblk,
                  pl.BlockSpec((1, qb, LANES), lambda i, j: (i, j, 0)),
                  pl.BlockSpec((1, qb, MIX_W), lambda i, j: (i, j, 4)),
                  pl.BlockSpec((1, N_HEADS, s, HEAD_DIM), lambda i, j: (i, 0, 0, 0)),
                  pl.BlockSpec((1, N_HEADS, s, 2 * HEAD_DIM), lambda i, j: (i, 0, 0, 0)),
                  pl.BlockSpec((1, s, IDX_DIM), lambda i, j: (i, 0, 0)),
                  pl.BlockSpec((kc, kc), lambda i, j: (0, 0))],
        out_specs=pl.BlockSpec((1, qb, MIX_W), lambda i, j: (i, j, 0)),
        out_shape=jax.ShapeDtypeStruct((b, s, MIX_W), F32),
        scratch_shapes=[pltpu.VMEM((qb, s), I32), pltpu.VMEM((32, qb, LANES), I32),
                        pltpu.VMEM((N_HEADS, qb, LANES), F32),
                        pltpu.VMEM((N_HEADS, qb, LANES), F32)],
        compiler_params=_params("parallel", "arbitrary"),
        name="dsa",
    )(q, iq, c_ikw, c_main, k, v, ik, tri)


def _merge_kernel(x_ref, gpre_ref, gpost_ref, wg_ref, wb_ref, wo_ref, y0, y1, y2, y3, y4, o_ref):
    x = x_ref[...]
    hb = _rms(x, gpre_ref[...]).astype(BF16)
    merged = None
    for i, y_ref in enumerate((y0, y1, y2, y3, y4)):
        gate = _sigmoid(jnp.dot(hb, wg_ref[:, i * D_MODEL:(i + 1) * D_MODEL], preferred_element_type=F32))
        term = gate * jnp.dot(y_ref[...].astype(BF16), wb_ref[i], preferred_element_type=F32)
        merged = term if merged is None else merged + term
    out = jnp.dot(merged.astype(BF16), wo_ref[...], preferred_element_type=F32)
    o_ref[...] = x + _rms(out, gpost_ref[...])


def _merge(x2d, g_pre, g_post, w_gate, w_branch, w_out, ys, tm):
    t, d = x2d.shape
    tok = lambda m: pl.BlockSpec((tm, m), lambda i: (i, 0))
    full = lambda shape: pl.BlockSpec(shape, lambda i: (0,) * len(shape))
    return pl.pallas_call(
        _merge_kernel,
        grid=(t // tm,),
        in_specs=[tok(d), full((1, d)), full((1, d)), full((d, GATE_COLS)), full((N_BRANCH, MIX_W, d)),
                  full((d, d))] + [tok(MIX_W)] * N_BRANCH,
        out_specs=tok(d),
        out_shape=jax.ShapeDtypeStruct((t, d), F32),
        compiler_params=_params("parallel"),
        name="merge",
    )(x2d, g_pre.reshape(1, d), g_post.reshape(1, d), w_gate, w_branch, w_out, *ys)


def kernel(x, mem, positions, norm_pre, norm_post, norm_mem, w_in, rwkv_mu, rwkv_w0, rwkv_w2, rwkv_a0, rwkv_a2, rwkv_k_k, rwkv_k_a, rwkv_r_k, rwkv_lnx_w, rwkv_lnx_b, ret_gn_w, s5_lam_re, s5_lam_im, s5_log_dt, s5_b_re, s5_b_im, s5_c_re, s5_c_im, s5_d, s5_w_glu, w_mem_kv, w_branch, w_out):
    b, s, d = x.shape
    t = b * s
    n_mem = mem.shape[1]
    depth = w_in.shape[0]
    tm = min(256, t)
    cos, sin = _rope_tables(positions, min(512, s))
    o_dsa = RWKV_COLS
    o_ret = o_dsa + DSA_COLS
    o_s5 = o_ret + RET_COLS
    o_x = o_s5 + S5_COLS
    o_gate = o_x + XATT_COLS
    widths = (RWKV_COLS, 5 * MIX_W, LANES, RET_COLS, S5_COLS, XATT_COLS)
    for l in range(depth):
        w = w_in[l]
        o_idx = o_dsa + 4 * MIX_W
        o_g = o_idx + IDX_DIM + IDX_HEADS
        w_cat = _to_bf16(jnp.concatenate([
            w[:, :RWKV_COLS],
            w[:, o_dsa:o_idx], w[:, o_g:o_ret],
            w[:, o_idx:o_g], jnp.zeros((d, LANES - IDX_DIM - IDX_HEADS), w.dtype),
            w[:, o_ret:o_gate]], axis=1))
        x2 = x.reshape(t, d)
        c_rwkv, c_dsa, c_ikw, c_ret, c_s5, c_x = _norm_proj(x2, norm_pre[l], w_cat, widths, tm)
        (mem_kv,) = _norm_proj(mem.reshape(b * n_mem, d), norm_mem[l], _to_bf16(w_mem_kv[l]), (2 * MIX_W,),
                               min(256, b * n_mem))
        r3 = lambda a: a.reshape(b, s, a.shape[-1])
        y_rwkv = _rwkv(r3(c_rwkv), rwkv_mu[l], rwkv_w0[l], rwkv_w2[l], rwkv_a0[l], rwkv_a2[l], rwkv_k_k[l],
                       rwkv_k_a[l], rwkv_r_k[l], rwkv_lnx_w[l], rwkv_lnx_b[l], ts=min(256, s), ch=64)
        y_dsa = _dsa(r3(c_dsa), r3(c_ikw), cos, sin, qb=min(256, s), kc=min(512, s), tp=min(512, s))
        y_ret = _retention(r3(c_ret), cos, sin, ret_gn_w[l], tc=min(256, s))
        y_s5 = _s5(r3(c_s5), s5_lam_re[l], s5_lam_im[l], s5_log_dt[l], s5_b_re[l], s5_b_im[l], s5_c_re[l],
                   s5_c_im[l], s5_d[l], s5_w_glu[l], tl=min(128, s))
        y_x = _xatt(r3(c_x), mem_kv.reshape(b, n_mem, 2 * MIX_W), tq=min(512, s))
        ys = [y.reshape(t, MIX_W) for y in (y_rwkv, y_dsa, y_ret, y_s5, y_x)]
        x = _merge(x2, norm_pre[l], norm_post[l], _to_bf16(w[:, o_gate:]), _to_bf16(w_branch[l]),
                   _to_bf16(w_out[l]), ys, tm).reshape(b, s, d)
    return x
```

```python
import functools
import math

import numpy as np
import jax
import jax.numpy as jnp
from jax import lax
from jax.experimental import pallas as pl
from jax.experimental.pallas import tpu as pltpu

F32, BF16, I32 = jnp.float32, jnp.bfloat16, jnp.int32
HIGHEST = lax.Precision.HIGHEST

D_MODEL = 1024
HEAD_DIM = 64
N_HEADS = 4
MIX_W = 256
RWKV_LORA = 64
RWKV_COLS = 4 * MIX_W + 2 * RWKV_LORA
RWKV_LNX_EPS = 64e-5
IDX_HEADS = 4
IDX_DIM = 64
TOPK_MAX = 256
DSA_COLS = 4 * MIX_W + IDX_HEADS * IDX_DIM + IDX_DIM + IDX_HEADS
RET_COLS = 4 * MIX_W
S5_GROUP = 16
S5_GROUPS = MIX_W // S5_GROUP
S5_STATE = 64
S5_COLS = 2 * MIX_W
XATT_COLS = 2 * MIX_W
N_BRANCH = 5
GATE_COLS = N_BRANCH * D_MODEL
ROPE_THETA = 10000.0
NORM_EPS = 1e-6

LANES = 128
VMEM_LIMIT = 48 * 1024 * 1024
INT_MIN = -(2 ** 31)

_NN = (((1,), (0,)), ((), ()))
_NT = (((1,), (1,)), ((), ()))
_TN = (((0,), (0,)), ((), ()))


def _mm(a, b, dims=_NN):
    return lax.dot_general(a.astype(BF16), b.astype(BF16), dims, preferred_element_type=F32)


def _split(a):
    hi = a.astype(BF16)
    return hi, (a - hi.astype(F32)).astype(BF16)


def _mm3s(a, b, dims=_NN):
    d = lambda x, y: lax.dot_general(x, y, dims, preferred_element_type=F32)
    return d(a[0], b[0]) + (d(a[0], b[1]) + d(a[1], b[0]))


def _mm3(a, b, dims=_NN):
    return _mm3s(_split(a), _split(b), dims)


def _split3(x):
    hi = x.astype(BF16)
    r1 = x - hi.astype(F32)
    mid = r1.astype(BF16)
    return hi, mid, (r1 - mid.astype(F32)).astype(BF16)


def _mm_ones_rhs(x, ones_bf16):
    d = lambda p: jnp.dot(p, ones_bf16, preferred_element_type=F32)
    hi, mid, lo = _split3(x)
    return d(hi) + (d(mid) + d(lo))


def _mm_ones_lhs(ones_bf16, x):
    d = lambda p: jnp.dot(ones_bf16, p, preferred_element_type=F32)
    hi, mid, lo = _split3(x)
    return d(hi) + (d(mid) + d(lo))


def _sigmoid(x):
    return 1.0 / (1.0 + jnp.exp(-x))


def _silu(x):
    return x * _sigmoid(x)


def _rms(x, g):
    return x * lax.rsqrt(jnp.mean(x * x, axis=-1, keepdims=True) + NORM_EPS) * g


def _head_ones(n):
    r = lax.broadcasted_iota(I32, (n, n), 0) >> 6
    c = lax.broadcasted_iota(I32, (n, n), 1) >> 6
    return jnp.where(r == c, 1.0, 0.0).astype(BF16)


def _head_norm(y, bd, eps):
    mu = _mm_ones_rhs(y, bd) * (1.0 / HEAD_DIM)
    d = y - mu
    var = _mm_ones_rhs(d * d, bd) * (1.0 / HEAD_DIM)
    return d * lax.rsqrt(var + eps)


def _bit_transpose32(a):
    a = list(a)
    j, m = 16, 0x0000FFFF
    while j:
        k = 0
        while k < 32:
            t = (a[k] ^ lax.shift_right_logical(a[k + j], jnp.full_like(a[k + j], j))) & m
            a[k] = a[k] ^ t
            a[k + j] = a[k + j] ^ (t << j)
            k = (k + j + 1) & ~j
        j >>= 1
        m = m ^ (m << j)
    return a


def _rope(x, cos, sin_signed):
    n = x.shape[-1]
    lane = lax.broadcasted_iota(I32, x.shape, x.ndim - 1)
    first_half = (lane & (HEAD_DIM - 1)) < HEAD_DIM // 2
    partner = jnp.where(first_half, pltpu.roll(x, n - HEAD_DIM // 2, axis=x.ndim - 1),
                        pltpu.roll(x, HEAD_DIM // 2, axis=x.ndim - 1))
    return x * cos + partner * sin_signed


def _params(*sem):
    return pltpu.CompilerParams(dimension_semantics=sem, vmem_limit_bytes=VMEM_LIMIT)


def _norm_proj_kernel(x_ref, g_ref, w_ref, *o_refs, widths):
    hb = _rms(x_ref[...], g_ref[...]).astype(BF16)
    off = 0
    for o_ref, n in zip(o_refs, widths):
        o_ref[...] = jnp.dot(hb, w_ref[:, off:off + n], preferred_element_type=F32)
        off += n


def _norm_proj(x2d, g, w_bf16, widths, tm):
    t, d = x2d.shape
    n = sum(widths)
    return pl.pallas_call(
        functools.partial(_norm_proj_kernel, widths=widths),
        grid=(t // tm,),
        in_specs=[pl.BlockSpec((tm, d), lambda i: (i, 0)),
                  pl.BlockSpec((1, d), lambda i: (0, 0)),
                  pl.BlockSpec((d, n), lambda i: (0, 0))],
        out_specs=[pl.BlockSpec((tm, w), lambda i: (i, 0)) for w in widths],
        out_shape=[jax.ShapeDtypeStruct((t, w), F32) for w in widths],
        compiler_params=_params("parallel"),
        name="norm_proj",
    )(x2d, g.reshape(1, d), w_bf16)


def _cast_kernel(x_ref, o_ref):
    o_ref[...] = x_ref[...].astype(o_ref.dtype)


def _to_bf16(w):
    w2 = w.reshape(-1, w.shape[-1])
    r, n = w2.shape
    tr = min(256, r)
    out = pl.pallas_call(
        _cast_kernel,
        grid=(r // tr,),
        in_specs=[pl.BlockSpec((tr, n), lambda i: (i, 0))],
        out_specs=pl.BlockSpec((tr, n), lambda i: (i, 0)),
        out_shape=jax.ShapeDtypeStruct((r, n), BF16),
        compiler_params=_params("parallel"),
        name="to_bf16",
    )(w2)
    return out.reshape(w.shape)


def _rope_table_kernel(pos_ref, inv_ref, sgn_ref, cos_ref, sin_ref):
    ang = pos_ref[0].astype(F32) * inv_ref[...]
    c = jnp.cos(ang)
    s = jnp.sin(ang) * sgn_ref[...]
    cos_ref[0] = jnp.concatenate([c, c], axis=-1)
    sin_ref[0] = jnp.concatenate([s, s], axis=-1)


def _rope_tables(positions, ts):
    b, s = positions.shape
    half = HEAD_DIM // 2
    inv = ROPE_THETA ** (-jnp.arange(half, dtype=F32) / half)
    inv = jnp.tile(inv, LANES // half).reshape(1, LANES)
    sgn = jnp.tile(jnp.concatenate([-jnp.ones((half,), F32), jnp.ones((half,), F32)]), LANES // HEAD_DIM)
    return pl.pallas_call(
        _rope_table_kernel,
        grid=(b, s // ts),
        in_specs=[pl.BlockSpec((1, ts, 1), lambda i, j: (i, j, 0)),
                  pl.BlockSpec((1, LANES), lambda i, j: (0, 0)),
                  pl.BlockSpec((1, LANES), lambda i, j: (0, 0))],
        out_specs=[pl.BlockSpec((1, ts, MIX_W), lambda i, j: (i, j, 0))] * 2,
        out_shape=[jax.ShapeDtypeStruct((b, s, MIX_W), F32)] * 2,
        compiler_params=_params("parallel", "parallel"),
        name="rope_tables",
    )(positions.reshape(b, s, 1), inv, sgn.reshape(1, LANES))


def _rwkv_kernel(c_ref, mu_ref, w0_ref, w2_ref, a0_ref, a2_ref, kk_ref, ka_ref, rk_ref, lnw_ref, lnb_ref,
                 o_ref, carry_ref, st_ref, y_ref, *, ts, ch):
    @pl.when(pl.program_id(1) == 0)
    def _():
        carry_ref[...] = jnp.zeros_like(carry_ref)
        st_ref[...] = jnp.zeros_like(st_ref)

    c = c_ref[0]
    row = lax.broadcasted_iota(I32, c.shape, 0)
    prev = jnp.where(row == 0, carry_ref[0:1, :], pltpu.roll(c, 1, axis=0))
    carry_ref[0:1, :] = c[ts - 1:ts, :]
    c = c + mu_ref[...] * (prev - c)
    w = MIX_W
    r, k, v = c[:, 0:w], c[:, w:2 * w], c[:, 2 * w:3 * w]
    wl, al = c[:, 3 * w:3 * w + RWKV_LORA], c[:, 3 * w + RWKV_LORA:3 * w + 2 * RWKV_LORA]
    g = c[:, 3 * w + 2 * RWKV_LORA:]

    z = w0_ref[...] + _mm3(jnp.tanh(wl), w2_ref[...])
    w_log = jnp.minimum(z, 0.0) - jnp.log(1.0 + jnp.exp(-jnp.abs(z))) - 0.5
    logw = -jnp.exp(w_log)
    a = _sigmoid(a0_ref[...] + _mm3(al, a2_ref[...]))
    bd = _head_ones(w)
    kk = k * kk_ref[...]
    kk = kk / jnp.maximum(jnp.sqrt(_mm_ones_rhs(kk * kk, bd)), 1e-12)
    kmod = k * (1.0 + (a - 1.0) * ka_ref[...])
    kka = kk * a

    shift = int(math.log2(ch))
    ri = lax.broadcasted_iota(I32, (ts, ts), 0)
    ci = lax.broadcasted_iota(I32, (ts, ts), 1)
    ltri = jnp.where((ci <= ri) & ((ri >> shift) == (ci >> shift)), 1.0, 0.0).astype(BF16)
    cum = _mm_ones_lhs(ltri, logw)
    e_in = jnp.exp(cum)
    a_hat = -kk * jnp.exp(cum - logw)
    r_hat = r * e_in
    e_inv = jnp.exp(-cum)
    b_hat = kka * e_inv
    k_hat = kmod * e_inv

    ti = lax.broadcasted_iota(I32, (2 * ch, ch), 0)
    si = lax.broadcasted_iota(I32, (2 * ch, ch), 1)
    tri2 = si < jnp.where(ti < ch, ti, ti - ch + 1)
    eye = jnp.where(lax.broadcasted_iota(I32, (ch, ch), 0) == lax.broadcasted_iota(I32, (ch, ch), 1), 1.0, 0.0)

    n_ch = ts // ch
    pairs = [(ci_, h) for ci_ in range(n_ch) for h in range(N_HEADS)]
    blk = lambda t, ci_, h: t[ci_ * ch:(ci_ + 1) * ch, h * HEAD_DIM:(h + 1) * HEAD_DIM]
    ar = [_split(jnp.concatenate([blk(a_hat, *p), blk(r_hat, *p)], axis=0)) for p in pairs]
    vhs = [blk(v, *p) for p in pairs]
    x_b = [jnp.where(tri2, _mm3s(ar[i], _split(blk(b_hat, *p)), _NT), 0.0) for i, p in enumerate(pairs)]
    x_k = [jnp.where(tri2, _mm3s(ar[i], _split(blk(k_hat, *p)), _NT), 0.0) for i, p in enumerate(pairs)]
    pw = [x[:ch] for x in x_b]
    tinv = [eye + p for p in pw]
    for _ in range(shift - 1):
        pws = [_split(p) for p in pw]
        pw = [_mm3s(p, p) for p in pws]
        tinv = [_mm3(t, eye + p) for t, p in zip(tinv, pw)]
    from_v = [_mm3(x, vh) for x, vh in zip(x_k, vhs)]

    st = [st_ref[h] for h in range(N_HEADS)]
    for ci_ in range(n_ch):
        lo, hi = ci_ * ch, (ci_ + 1) * ch
        cum_end = cum[hi - 1:hi, :]
        e_tail = jnp.exp(cum_end - cum[lo:hi, :])
        b_til = kka[lo:hi, :] * e_tail
        k_til = kmod[lo:hi, :] * e_tail
        p_end = jnp.exp(cum_end)
        ids = [ci_ * N_HEADS + h for h in range(N_HEADS)]
        hs = [slice(h * HEAD_DIM, (h + 1) * HEAD_DIM) for h in range(N_HEADS)]
        from_state = [_mm3s(ar[i], _split(st[h]), _NT) for h, i in enumerate(ids)]
        u = [_mm3(tinv[i], from_state[h][:ch] + from_v[i][:ch]) for h, i in enumerate(ids)]
        for h, i in enumerate(ids):
            y_ref[lo:hi, hs[h]] = from_state[h][ch:] + from_v[i][ch:] + _mm3(x_b[i][ch:], u[h])
        st = [st[h] * p_end[:, hs[h]] + _mm3(jnp.concatenate([u[h], vhs[i]], axis=0),
                                             jnp.concatenate([b_til[:, hs[h]], k_til[:, hs[h]]], axis=0), _TN)
              for h, i in enumerate(ids)]
    for h in range(N_HEADS):
        st_ref[h] = st[h]

    yn = _head_norm(y_ref[...], bd, RWKV_LNX_EPS) * lnw_ref[...] + lnb_ref[...]
    bonus = _mm_ones_rhs(r * kmod * rk_ref[...], bd) * v
    o_ref[0] = (yn + bonus) * _silu(g)


def _rwkv(c, mu, w0, w2, a0, a2, k_k, k_a, r_k, lnx_w, lnx_b, ts, ch):
    b, s, n = c.shape
    row = lambda t: t.reshape(1, -1)
    vec = lambda m: pl.BlockSpec((1, m), lambda i, j: (0, 0))
    mat = lambda m: pl.BlockSpec((RWKV_LORA, m), lambda i, j: (0, 0))
    return pl.pallas_call(
        functools.partial(_rwkv_kernel, ts=ts, ch=ch),
        grid=(b, s // ts),
        in_specs=[pl.BlockSpec((1, ts, n), lambda i, j: (i, j, 0)),
                  vec(n), vec(MIX_W), mat(MIX_W), vec(MIX_W), mat(MIX_W), vec(MIX_W), vec(MIX_W), vec(MIX_W),
                  vec(MIX_W), vec(MIX_W)],
        out_specs=pl.BlockSpec((1, ts, MIX_W), lambda i, j: (i, j, 0)),
        out_shape=jax.ShapeDtypeStruct((b, s, MIX_W), F32),
        scratch_shapes=[pltpu.VMEM((8, n), F32),
                        pltpu.VMEM((N_HEADS, HEAD_DIM, HEAD_DIM), F32),
                        pltpu.VMEM((ts, MIX_W), F32)],
        compiler_params=_params("parallel", "arbitrary"),
        name="rwkv7",
    )(c, row(mu), row(w0), w2, row(a0), a2, row(k_k), row(k_a), row(r_k), row(lnx_w), row(lnx_b))


def _ret_tables(tc):
    log_g = np.log(1.0 - np.exp(np.linspace(math.log(1.0 / 32), math.log(1.0 / 512), N_HEADS)))
    j = np.arange(tc, dtype=np.float64)
    rel = j[:, None] - j[None, :]
    dmask = np.where(rel >= 0, np.exp(log_g[:, None, None] * np.maximum(rel, 0.0)), 0.0)
    qdec = np.repeat(np.exp(log_g[None, :] * (j[:, None] + 1.0)), HEAD_DIM, axis=1)
    kdec = np.repeat(np.exp(log_g[None, :] * (tc - 1.0 - j[:, None])), HEAD_DIM, axis=1)
    cdec = np.repeat(np.exp(log_g * tc), HEAD_DIM)[None, :]
    f = lambda t: jnp.asarray(t, F32)
    return f(dmask), f(qdec), f(kdec), f(cdec)


def _ret_kernel(c_ref, cos_ref, sin_ref, dm_ref, qd_ref, kd_ref, cd_ref, gn_ref, o_ref, r_ref, y_ref):
    @pl.when(pl.program_id(1) == 0)
    def _():
        r_ref[...] = jnp.zeros_like(r_ref)

    w = MIX_W
    c = c_ref[0]
    cos, sin = cos_ref[0], sin_ref[0]
    q = _rope(c[:, 0:w], cos, sin)
    k = _rope(c[:, w:2 * w], cos, sin) * (HEAD_DIM ** -0.5)
    v = c[:, 2 * w:3 * w]
    g = c[:, 3 * w:]
    kd = k * kd_ref[...]
    qd = qd_ref[...]
    cd = cd_ref[...]
    for h in range(N_HEADS):
        ls = slice(h * HEAD_DIM, (h + 1) * HEAD_DIM)
        qh, kh, vh = q[:, ls], k[:, ls], v[:, ls]
        rs = r_ref[h]
        att = _mm(qh, kh, _NT) * dm_ref[h]
        y_ref[:, ls] = _mm(att, vh) + _mm(qh, rs) * qd[:, ls]
        r_ref[h] = rs * cd[:, ls] + _mm(kd[:, ls], vh, _TN)
    bd = _head_ones(w)
    o_ref[0] = _head_norm(y_ref[...], bd, NORM_EPS) * gn_ref[...] * _silu(g)


def _retention(c, cos, sin, gn_w, tc):
    b, s, n = c.shape
    dmask, qdec, kdec, cdec = _ret_tables(tc)
    tok = lambda m: pl.BlockSpec((1, tc, m), lambda i, j: (i, j, 0))
    full = lambda shape: pl.BlockSpec(shape, lambda i, j: (0,) * len(shape))
    return pl.pallas_call(
        _ret_kernel,
        grid=(b, s // tc),
        in_specs=[tok(n), tok(MIX_W), tok(MIX_W), full((N_HEADS, tc, tc)), full((tc, MIX_W)), full((tc, MIX_W)),
                  full((1, MIX_W)), full((1, MIX_W))],
        out_specs=tok(MIX_W),
        out_shape=jax.ShapeDtypeStruct((b, s, MIX_W), F32),
        scratch_shapes=[pltpu.VMEM((N_HEADS, HEAD_DIM, HEAD_DIM), F32), pltpu.VMEM((tc, MIX_W), F32)],
        compiler_params=_params("parallel", "arbitrary"),
        name="retention",
    )(c, cos, sin, dmask, qdec, kdec, cdec, gn_w.reshape(1, MIX_W))


def _s5_tables(lam_re, lam_im, log_dt, b_re, b_im, c_re, c_im, tl):
    gn, p = lam_re.shape
    lr = jnp.minimum(lam_re.astype(F32), -1e-4)
    li = lam_im.astype(F32)
    dt = jnp.exp(log_dt.astype(F32))[:, None]
    mag = jnp.exp(lr * dt)
    ab_re, ab_im = mag * jnp.cos(li * dt), mag * jnp.sin(li * dt)
    den = lr * lr + li * li
    f_re = ((ab_re - 1.0) * lr + ab_im * li) / den
    f_im = (ab_im * lr - (ab_re - 1.0) * li) / den
    bb_re = f_re[..., None] * b_re - f_im[..., None] * b_im
    bb_im = f_re[..., None] * b_im + f_im[..., None] * b_re
    eye = jnp.eye(gn, dtype=F32)
    bd_in = lambda bb: jnp.einsum('gpc,gh->gchp', bb, eye).reshape(gn * S5_GROUP, gn * p)
    w_in = jnp.concatenate([bd_in(bb_re), bd_in(bb_im)], axis=1)
    bd_out = lambda cc: jnp.einsum('gcp,gh->gphc', cc, eye).reshape(gn * p, gn * S5_GROUP)
    w_out = jnp.concatenate([bd_out(c_re), -bd_out(c_im)], axis=0)
    def powers(n):
        n = n.astype(F32)[:, None, None]
        m = jnp.exp(n * (lr * dt)[None])
        ang = n * (li * dt)[None]
        return (m * jnp.cos(ang)).reshape(-1, gn * p), (m * jnp.sin(ang)).reshape(-1, gn * p)
    steps = 2 ** jnp.arange(int(math.log2(tl)))
    dbl_re, dbl_im = powers(steps)
    car_re, car_im = powers(jnp.arange(1, tl + 1))
    return w_in, w_out, dbl_re, dbl_im, car_re, car_im


def _s5_kernel(c_ref, win_ref, wout_ref, dr_ref, di_ref, cr_ref, ci_ref, dskip_ref, wglu_ref, o_ref,
               xr_ref, xi_ref, *, tl):
    @pl.when(pl.program_id(1) == 0)
    def _():
        xr_ref[...] = jnp.zeros_like(xr_ref)
        xi_ref[...] = jnp.zeros_like(xi_ref)

    ns = S5_GROUPS * S5_STATE
    c = c_ref[0]
    u, g = c[:, :MIX_W], c[:, MIX_W:]
    bu = _mm(u, win_ref[...])
    xr, xi = bu[:, :ns], bu[:, ns:]
    row = lax.broadcasted_iota(I32, (tl, ns), 0)
    for j in range(int(math.log2(tl))):
        d = 1 << j
        ar, ai = dr_ref[j:j + 1, :], di_ref[j:j + 1, :]
        sr = jnp.where(row >= d, pltpu.roll(xr, d, axis=0), 0.0)
        si = jnp.where(row >= d, pltpu.roll(xi, d, axis=0), 0.0)
        xr, xi = xr + ar * sr - ai * si, xi + ar * si + ai * sr
    pr, pi = cr_ref[...], ci_ref[...]
    x0r, x0i = xr_ref[0:1, :], xi_ref[0:1, :]
    xr, xi = xr + pr * x0r - pi * x0i, xi + pr * x0i + pi * x0r
    xr_ref[0:1, :] = xr[tl - 1:tl, :]
    xi_ref[0:1, :] = xi[tl - 1:tl, :]
    y = _mm(jnp.concatenate([xr, xi], axis=-1), wout_ref[...]) + dskip_ref[...] * u
    y = y * (0.5 * (1.0 + jnp.tanh(math.sqrt(2.0 / math.pi) * (y + 0.044715 * (y * y * y)))))
    y = y * _sigmoid(_mm(y, wglu_ref[...]))
    o_ref[0] = y * _silu(g)


def _s5(c, lam_re, lam_im, log_dt, b_re, b_im, c_re, c_im, d_skip, w_glu, tl):
    b, s, n = c.shape
    ns = S5_GROUPS * S5_STATE
    w_in, w_out, dbl_re, dbl_im, car_re, car_im = _s5_tables(lam_re, lam_im, log_dt, b_re, b_im, c_re, c_im, tl)
    nd = dbl_re.shape[0]
    tok = lambda m: pl.BlockSpec((1, tl, m), lambda i, j: (i, j, 0))
    full = lambda shape: pl.BlockSpec(shape, lambda i, j: (0,) * len(shape))
    return pl.pallas_call(
        functools.partial(_s5_kernel, tl=tl),
        grid=(b, s // tl),
        in_specs=[tok(n), full((MIX_W, 2 * ns)), full((2 * ns, MIX_W)), full((nd, ns)), full((nd, ns)),
                  full((tl, ns)), full((tl, ns)), full((1, MIX_W)), full((MIX_W, MIX_W))],
        out_specs=tok(MIX_W),
        out_shape=jax.ShapeDtypeStruct((b, s, MIX_W), F32),
        scratch_shapes=[pltpu.VMEM((8, ns), F32), pltpu.VMEM((8, ns), F32)],
        compiler_params=_params("parallel", "arbitrary"),
        name="s5",
    )(c, w_in.astype(BF16), w_out.astype(BF16), dbl_re, dbl_im, car_re, car_im, d_skip.reshape(1, MIX_W),
      w_glu.astype(BF16))


def _xatt_kernel(c_ref, kv_ref, o_ref, y_ref):
    w = MIX_W
    c = c_ref[0]
    q, g = c[:, :w], c[:, w:]
    kv = kv_ref[0]
    for h in range(N_HEADS):
        ls = slice(h * HEAD_DIM, (h + 1) * HEAD_DIM)
        lg = _mm(q[:, ls], kv[:, ls], _NT) * (HEAD_DIM ** -0.5)
        p = jnp.exp(lg - jnp.max(lg, axis=-1, keepdims=True))
        y_ref[:, ls] = _mm(p, kv[:, w + h * HEAD_DIM:w + (h + 1) * HEAD_DIM]) / jnp.sum(p, axis=-1, keepdims=True)
    o_ref[0] = y_ref[...] * _silu(g)


def _xatt(c, mem_kv, tq):
    b, s, n = c.shape
    nm = mem_kv.shape[1]
    return pl.pallas_call(
        _xatt_kernel,
        grid=(b, s // tq),
        in_specs=[pl.BlockSpec((1, tq, n), lambda i, j: (i, j, 0)),
                  pl.BlockSpec((1, nm, 2 * MIX_W), lambda i, j: (i, 0, 0))],
        out_specs=pl.BlockSpec((1, tq, MIX_W), lambda i, j: (i, j, 0)),
        out_shape=jax.ShapeDtypeStruct((b, s, MIX_W), F32),
        scratch_shapes=[pltpu.VMEM((tq, MIX_W), F32)],
        compiler_params=_params("parallel", "parallel"),
        name="mem_xatt",
    )(c, mem_kv)


def _dsa_prep_kernel(c_ref, ikw_ref, cos_ref, sin_ref, q_ref, k_ref, v_ref, iq_ref, ik_ref):
    w = MIX_W
    c = c_ref[0]
    cos, sin = cos_ref[0], sin_ref[0]
    q = _rope(c[:, 0:w], cos, sin) * (HEAD_DIM ** -0.5)
    k = _rope(c[:, w:2 * w], cos, sin)
    v = c[:, 2 * w:3 * w]
    iq = _rope(c[:, 3 * w:4 * w], cos, sin) * (IDX_DIM ** -0.5)
    ik = _rope(ikw_ref[0], cos[:, :LANES], sin[:, :LANES])
    one_col = jnp.where(lax.broadcasted_iota(I32, (c.shape[0], HEAD_DIM), 1) == 0, 1.0, 0.0)
    for h in range(N_HEADS):
        ls = slice(h * HEAD_DIM, (h + 1) * HEAD_DIM)
        q_ref[0, h] = q[:, ls].astype(BF16)
        k_ref[0, h] = k[:, ls].astype(BF16)
        v_ref[0, h] = jnp.concatenate([v[:, ls], one_col], axis=-1).astype(BF16)
        iq_ref[0, h] = iq[:, ls].astype(BF16)
    ik_ref[0] = ik[:, :IDX_DIM].astype(BF16)


def _dsa_prep(c_main, c_ikw, cos, sin, tp):
    b, s, _ = c_main.shape
    tok = lambda m: pl.BlockSpec((1, tp, m), lambda i, j: (i, j, 0))
    hm = lambda m: pl.BlockSpec((1, N_HEADS, tp, m), lambda i, j: (i, 0, j, 0))
    hm_shape = lambda m: jax.ShapeDtypeStruct((b, N_HEADS, s, m), BF16)
    return pl.pallas_call(
        _dsa_prep_kernel,
        grid=(b, s // tp),
        in_specs=[pl.BlockSpec((1, tp, 4 * MIX_W), lambda i, j: (i, j, 0)), tok(LANES), tok(MIX_W), tok(MIX_W)],
        out_specs=[hm(HEAD_DIM), hm(HEAD_DIM), hm(2 * HEAD_DIM), hm(HEAD_DIM), tok(IDX_DIM)],
        out_shape=[hm_shape(HEAD_DIM), hm_shape(HEAD_DIM), hm_shape(2 * HEAD_DIM), hm_shape(HEAD_DIM),
                   jax.ShapeDtypeStruct((b, s, IDX_DIM), BF16)],
        compiler_params=_params("parallel", "parallel"),
        name="dsa_prep",
    )(c_main, c_ikw, cos, sin)


def _dsa_kernel(q_ref, iq_ref, ikw_ref, g_ref, k_ref, v_ref, ik_ref, tri_ref, o_ref, key_ref, planes_ref, m_ref,
                acc_ref, *, qb, kc, n_sel, n_tile):
    q0 = pl.program_id(1) * qb
    nkc = (q0 + qb + kc - 1) // kc
    qidx = q0 + lax.broadcasted_iota(I32, (qb, kc), 0)
    lane = lax.broadcasted_iota(I32, (qb, kc), 1)
    wide = lambda t: jnp.concatenate([t] * (kc // LANES), axis=1)
    ones_k = jnp.ones((kc, LANES), BF16)
    chunk = lambda c: pl.ds(pl.multiple_of(c * kc, kc), kc)
    iw = ikw_ref[0] * (IDX_HEADS ** -0.5)
    iws = [iw[:, IDX_DIM + h:IDX_DIM + h + 1] for h in range(IDX_HEADS)]

    def score_body(c, carry):
        ikc = ik_ref[0, chunk(c), :]
        acc = jnp.zeros((qb, kc), F32)
        for h in range(IDX_HEADS):
            s = lax.dot_general(iq_ref[0, h], ikc, _NT, preferred_element_type=F32)
            acc = acc + jnp.maximum(s, 0.0) * iws[h]
        acc = jnp.where(acc == 0.0, 0.0, acc)
        bits = pltpu.bitcast(acc, I32)
        key = jnp.where(bits < 0, bits ^ 0x7FFFFFFF, bits)
        key_ref[:, chunk(c)] = jnp.where(c * kc + lane <= qidx, key, INT_MIN)
        return carry

    lax.fori_loop(0, nkc, score_body, 0)

    def fill_body(c, carry):
        key_ref[:, chunk(c)] = jnp.full((qb, kc), INT_MIN, I32)
        return carry

    lax.fori_loop(nkc, n_tile * LANES // kc, fill_body, 0)

    def plane_body(gidx, carry):
        rows = pl.ds(pl.multiple_of(gidx * 8, 8), 8)
        tiles = [key_ref[rows, i * LANES:(i + 1) * LANES] ^ INT_MIN for i in range(n_tile)]
        tiles += [jnp.zeros((8, LANES), I32)] * (32 - n_tile)
        for p, plane in enumerate(_bit_transpose32(tiles)):
            planes_ref[p, rows, :] = plane
        return carry

    lax.fori_loop(0, qb // 8, plane_body, 0)

    ones_l = jnp.ones((LANES, LANES), BF16)

    def bit_body(t, carry):
        alive, above, vbits = carry
        p = 2 * t
        hi, lo = planes_ref[p], planes_ref[p + 1]
        a1 = alive & hi
        a0 = alive ^ a1
        o11 = a1 & lo
        o10 = a1 ^ o11
        o01 = a0 & lo
        pc = jnp.concatenate([lax.population_count(o).astype(F32).astype(BF16) for o in (o11, o10, o01)], axis=0)
        cnt = jnp.dot(pc, ones_l, preferred_element_type=F32)
        s1 = above + cnt[:qb]
        s2 = s1 + cnt[qb:2 * qb]
        s3 = s2 + cnt[2 * qb:]
        t11, t10, t01 = s1 >= n_sel, s2 >= n_sel, s3 >= n_sel
        alive = jnp.where(t11, o11, jnp.where(t10, o10, jnp.where(t01, o01, a0 ^ o01)))
        above = jnp.where(t11, above, jnp.where(t10, s1, jnp.where(t01, s2, s3)))
        digit = jnp.where(t11, 3, jnp.where(t10, 2, jnp.where(t01, 1, 0)))
        return alive, above, vbits | jnp.left_shift(digit, 30 - p)

    _, above, vbits = lax.fori_loop(
        0, 16, bit_body, (jnp.full((qb, LANES), -1, I32), jnp.zeros((qb, LANES), F32), jnp.zeros((qb, LANES), I32)))
    vs = wide(jnp.maximum(vbits ^ INT_MIN, INT_MIN + 1))
    need = wide(n_sel - above)

    def bias_body(c, run):
        kv = key_ref[:, chunk(c)]
        tied = kv == vs
        tied_b = jnp.where(tied, 1.0, 0.0).astype(BF16)
        rank = wide(run) + jnp.dot(tied_b, tri_ref[...], preferred_element_type=F32)
        bias = jnp.where(kv > vs, 0.0, jnp.where(tied, jnp.where(rank <= need, 0.0, -1e30), -1e30))
        key_ref[:, chunk(c)] = pltpu.bitcast(bias, I32)
        return run + jnp.dot(tied_b, ones_k, preferred_element_type=F32)

    lax.fori_loop(0, nkc, bias_body, jnp.zeros((qb, LANES), F32))

    m_ref[...] = jnp.full(m_ref.shape, -1e30, F32)
    acc_ref[...] = jnp.zeros(acc_ref.shape, F32)

    def att_body(c, carry):
        bias = pltpu.bitcast(key_ref[:, chunk(c)], F32)
        for h in range(N_HEADS):
            lg = lax.dot_general(q_ref[0, h], k_ref[0, h, chunk(c), :], _NT, preferred_element_type=F32) + bias
            m_old = m_ref[h]
            m_new = jnp.maximum(m_old, jnp.broadcast_to(jnp.max(lg, axis=-1, keepdims=True), (qb, LANES)))
            p = jnp.exp(lg - wide(m_new)).astype(BF16)
            acc_ref[h] = jnp.exp(m_old - m_new) * acc_ref[h] + jnp.dot(p, v_ref[0, h, chunk(c), :],
                                                                       preferred_element_type=F32)
            m_ref[h] = m_new
        return carry

    lax.fori_loop(0, nkc, att_body, 0)
    y = jnp.concatenate([acc_ref[h][:, :HEAD_DIM] / acc_ref[h][:, HEAD_DIM:HEAD_DIM + 1] for h in range(N_HEADS)],
                        axis=-1)
    o_ref[0] = y * _silu(g_ref[0])


def _dsa(c_main, c_ikw, cos, sin, qb, kc, tp):
    b, s, _ = c_main.shape
    q, k, v, iq, ik = _dsa_prep(c_main, c_ikw, cos, sin, tp)
    n_sel = min(TOPK_MAX, s // 4)
    n_tile = s // LANES
    assert n_tile <= 32 and s % kc == 0, "one key per bit of a 32-bit plane word"
    tri = jnp.asarray(np.triu(np.ones((kc, kc), np.float32)), BF16)
    qblk = pl.BlockSpec((1, N_HEADS, qb, HEAD_DIM), lambda i, j: (i, 0, j, 0))
    return pl.pallas_call(
        functools.partial(_dsa_kernel, qb=qb, kc=kc, n_sel=n_sel, n_tile=n_tile),
        grid=(b, s // qb),
        in_specs=[qblk, qblk,
                  pl.BlockSpec((1, qb, LANES), lambda i, j: (i, j, 0)),
                  pl.BlockSpec((1, qb, MIX_W), lambda i, j: (i, j, 4)),
                  pl.BlockSpec((1, N_HEADS, s, HEAD_DIM), lambda i, j: (i, 0, 0, 0)),
                  pl.BlockSpec((1, N_HEADS, s, 2 * HEAD_DIM), lambda i, j: (i, 0, 0, 0)),
                  pl.BlockSpec((1, s, IDX_DIM), lambda i, j: (i, 0, 0)),
                  pl.BlockSpec((kc, kc), lambda i, j: (0, 0))],
        out_specs=pl.BlockSpec((1, qb, MIX_W), lambda i, j: (i, j, 0)),
        out_shape=jax.ShapeDtypeStruct((b, s, MIX_W), F32),
        scratch_shapes=[pltpu.VMEM((qb, s), I32), pltpu.VMEM((32, qb, LANES), I32),
                        pltpu.VMEM((N_HEADS, qb, LANES), F32),
                        pltpu.VMEM((N_HEADS, qb, LANES), F32)],
        compiler_params=_params("parallel", "arbitrary"),
        name="dsa",
    )(q, iq, c_ikw, c_main, k, v, ik, tri)


def _merge_kernel(x_ref, gpre_ref, gpost_ref, wg_ref, wb_ref, wo_ref, y0, y1, y2, y3, y4, o_ref):
    x = x_ref[...]
    hb = _rms(x, gpre_ref[...]).astype(BF16)
    merged = None
    for i, y_ref in enumerate((y0, y1, y2, y3, y4)):
        gate = _sigmoid(jnp.dot(hb, wg_ref[:, i * D_MODEL:(i + 1) * D_MODEL], preferred_element_type=F32))
        term = gate * jnp.dot(y_ref[...].astype(BF16), wb_ref[i], preferred_element_type=F32)
        merged = term if merged is None else merged + term
    out = jnp.dot(merged.astype(BF16), wo_ref[...], preferred_element_type=F32)
    o_ref[...] = x + _rms(out, gpost_ref[...])


def _merge(x2d, g_pre, g_post, w_gate, w_branch, w_out, ys, tm):
    t, d = x2d.shape
    tok = lambda m: pl.BlockSpec((tm, m), lambda i: (i, 0))
    full = lambda shape: pl.BlockSpec(shape, lambda i: (0,) * len(shape))
    return pl.pallas_call(
        _merge_kernel,
        grid=(t // tm,),
        in_specs=[tok(d), full((1, d)), full((1, d)), full((d, GATE_COLS)), full((N_BRANCH, MIX_W, d)),
                  full((d, d))] + [tok(MIX_W)] * N_BRANCH,
        out_specs=tok(d),
        out_shape=jax.ShapeDtypeStruct((t, d), F32),
        compiler_params=_params("parallel"),
        name="merge",
    )(x2d, g_pre.reshape(1, d), g_post.reshape(1, d), w_gate, w_branch, w_out, *ys)


def kernel(x, mem, positions, norm_pre, norm_post, norm_mem, w_in, rwkv_mu, rwkv_w0, rwkv_w2, rwkv_a0, rwkv_a2, rwkv_k_k, rwkv_k_a, rwkv_r_k, rwkv_lnx_w, rwkv_lnx_b, ret_gn_w, s5_lam_re, s5_lam_im, s5_log_dt, s5_b_re, s5_b_im, s5_c_re, s5_c_im, s5_d, s5_w_glu, w_mem_kv, w_branch, w_out):
    b, s, d = x.shape
    t = b * s
    n_mem = mem.shape[1]
    depth = w_in.shape[0]
    tm = min(256, t)
    cos, sin = _rope_tables(positions, min(512, s))
    o_dsa = RWKV_COLS
    o_ret = o_dsa + DSA_COLS
    o_s5 = o_ret + RET_COLS
    o_x = o_s5 + S5_COLS
    o_gate = o_x + XATT_COLS
    widths = (RWKV_COLS, 5 * MIX_W, LANES, RET_COLS, S5_COLS, XATT_COLS)
    for l in range(depth):
        w = w_in[l]
        o_idx = o_dsa + 4 * MIX_W
        o_g = o_idx + IDX_DIM + IDX_HEADS
        w_cat = _to_bf16(jnp.concatenate([
            w[:, :RWKV_COLS],
            w[:, o_dsa:o_idx], w[:, o_g:o_ret],
            w[:, o_idx:o_g], jnp.zeros((d, LANES - IDX_DIM - IDX_HEADS), w.dtype),
            w[:, o_ret:o_gate]], axis=1))
        x2 = x.reshape(t, d)
        c_rwkv, c_dsa, c_ikw, c_ret, c_s5, c_x = _norm_proj(x2, norm_pre[l], w_cat, widths, tm)
        (mem_kv,) = _norm_proj(mem.reshape(b * n_mem, d), norm_mem[l], _to_bf16(w_mem_kv[l]), (2 * MIX_W,),
                               min(256, b * n_mem))
        r3 = lambda a: a.reshape(b, s, a.shape[-1])
        y_rwkv = _rwkv(r3(c_rwkv), rwkv_mu[l], rwkv_w0[l], rwkv_w2[l], rwkv_a0[l], rwkv_a2[l], rwkv_k_k[l],
                       rwkv_k_a[l], rwkv_r_k[l], rwkv_lnx_w[l], rwkv_lnx_b[l], ts=min(256, s), ch=64)
        y_dsa = _dsa(r3(c_dsa), r3(c_ikw), cos, sin, qb=min(256, s), kc=min(512, s), tp=min(512, s))
        y_ret = _retention(r3(c_ret), cos, sin, ret_gn_w[l], tc=min(256, s))
        y_s5 = _s5(r3(c_s5), s5_lam_re[l], s5_lam_im[l], s5_log_dt[l], s5_b_re[l], s5_b_im[l], s5_c_re[l],
                   s5_c_im[l], s5_d[l], s5_w_glu[l], tl=min(128, s))
        y_x = _xatt(r3(c_x), mem_kv.reshape(b, n_mem, 2 * MIX_W), tq=min(512, s))
        ys = [y.reshape(t, MIX_W) for y in (y_rwkv, y_dsa, y_ret, y_s5, y_x)]
        x = _merge(x2, norm_pre[l], norm_post[l], _to_bf16(w[:, o_gate:]), _to_bf16(w_branch[l]),
                   _to_bf16(w_out[l]), ys, tm).reshape(b, s, d)
    return x
```

```python
import functools
import math

import numpy as np
import jax
import jax.numpy as jnp
from jax import lax
from jax.experimental import pallas as pl
from jax.experimental.pallas import tpu as pltpu

F32, BF16, I32 = jnp.float32, jnp.bfloat16, jnp.int32
HIGHEST = lax.Precision.HIGHEST

D_MODEL = 1024
HEAD_DIM = 64
N_HEADS = 4
MIX_W = 256
RWKV_LORA = 64
RWKV_COLS = 4 * MIX_W + 2 * RWKV_LORA
RWKV_LNX_EPS = 64e-5
IDX_HEADS = 4
IDX_DIM = 64
TOPK_MAX = 256
DSA_COLS = 4 * MIX_W + IDX_HEADS * IDX_DIM + IDX_DIM + IDX_HEADS
RET_COLS = 4 * MIX_W
S5_GROUP = 16
S5_GROUPS = MIX_W // S5_GROUP
S5_STATE = 64
S5_COLS = 2 * MIX_W
XATT_COLS = 2 * MIX_W
N_BRANCH = 5
GATE_COLS = N_BRANCH * D_MODEL
ROPE_THETA = 10000.0
NORM_EPS = 1e-6

LANES = 128
VMEM_LIMIT = 48 * 1024 * 1024
INT_MIN = -(2 ** 31)

_NN = (((1,), (0,)), ((), ()))
_NT = (((1,), (1,)), ((), ()))
_TN = (((0,), (0,)), ((), ()))


def _mm(a, b, dims=_NN):
    return lax.dot_general(a.astype(BF16), b.astype(BF16), dims, preferred_element_type=F32)


def _split(a):
    hi = a.astype(BF16)
    return hi, (a - hi.astype(F32)).astype(BF16)


def _mm3s(a, b, dims=_NN):
    d = lambda x, y: lax.dot_general(x, y, dims, preferred_element_type=F32)
    return d(a[0], b[0]) + (d(a[0], b[1]) + d(a[1], b[0]))


def _mm3(a, b, dims=_NN):
    return _mm3s(_split(a), _split(b), dims)


def _split3(x):
    hi = x.astype(BF16)
    r1 = x - hi.astype(F32)
    mid = r1.astype(BF16)
    return hi, mid, (r1 - mid.astype(F32)).astype(BF16)


def _mm_ones_rhs(x, ones_bf16):
    d = lambda p: jnp.dot(p, ones_bf16, preferred_element_type=F32)
    hi, mid, lo = _split3(x)
    return d(hi) + (d(mid) + d(lo))


def _mm_ones_lhs(ones_bf16, x):
    d = lambda p: jnp.dot(ones_bf16, p, preferred_element_type=F32)
    hi, mid, lo = _split3(x)
    return d(hi) + (d(mid) + d(lo))


def _sigmoid(x):
    return 1.0 / (1.0 + jnp.exp(-x))


def _silu(x):
    return x * _sigmoid(x)


def _rms(x, g):
    return x * lax.rsqrt(jnp.mean(x * x, axis=-1, keepdims=True) + NORM_EPS) * g


def _head_ones(n):
    r = lax.broadcasted_iota(I32, (n, n), 0) >> 6
    c = lax.broadcasted_iota(I32, (n, n), 1) >> 6
    return jnp.where(r == c, 1.0, 0.0).astype(BF16)


def _head_norm(y, bd, eps):
    mu = _mm_ones_rhs(y, bd) * (1.0 / HEAD_DIM)
    d = y - mu
    var = _mm_ones_rhs(d * d, bd) * (1.0 / HEAD_DIM)
    return d * lax.rsqrt(var + eps)


def _bit_transpose32(a):
    a = list(a)
    j, m = 16, 0x0000FFFF
    while j:
        k = 0
        while k < 32:
            t = (a[k] ^ lax.shift_right_logical(a[k + j], jnp.full_like(a[k + j], j))) & m
            a[k] = a[k] ^ t
            a[k + j] = a[k + j] ^ (t << j)
            k = (k + j + 1) & ~j
        j >>= 1
        m = m ^ (m << j)
    return a


def _rope(x, cos, sin_signed):
    n = x.shape[-1]
    lane = lax.broadcasted_iota(I32, x.shape, x.ndim - 1)
    first_half = (lane & (HEAD_DIM - 1)) < HEAD_DIM // 2
    partner = jnp.where(first_half, pltpu.roll(x, n - HEAD_DIM // 2, axis=x.ndim - 1),
                        pltpu.roll(x, HEAD_DIM // 2, axis=x.ndim - 1))
    return x * cos + partner * sin_signed


def _params(*sem):
    return pltpu.CompilerParams(dimension_semantics=sem, vmem_limit_bytes=VMEM_LIMIT)


def _norm_proj_kernel(x_ref, g_ref, w_ref, *o_refs, widths):
    hb = _rms(x_ref[...], g_ref[...]).astype(BF16)
    off = 0
    for o_ref, n in zip(o_refs, widths):
        o_ref[...] = jnp.dot(hb, w_ref[:, off:off + n], preferred_element_type=F32)
        off += n


def _norm_proj(x2d, g, w_bf16, widths, tm):
    t, d = x2d.shape
    n = sum(widths)
    return pl.pallas_call(
        functools.partial(_norm_proj_kernel, widths=widths),
        grid=(t // tm,),
        in_specs=[pl.BlockSpec((tm, d), lambda i: (i, 0)),
                  pl.BlockSpec((1, d), lambda i: (0, 0)),
                  pl.BlockSpec((d, n), lambda i: (0, 0))],
        out_specs=[pl.BlockSpec((tm, w), lambda i: (i, 0)) for w in widths],
        out_shape=[jax.ShapeDtypeStruct((t, w), F32) for w in widths],
        compiler_params=_params("parallel"),
        name="norm_proj",
    )(x2d, g.reshape(1, d), w_bf16)


def _cast_kernel(x_ref, o_ref):
    o_ref[...] = x_ref[...].astype(o_ref.dtype)


def _to_bf16(w):
    w2 = w.reshape(-1, w.shape[-1])
    r, n = w2.shape
    tr = min(256, r)
    out = pl.pallas_call(
        _cast_kernel,
        grid=(r // tr,),
        in_specs=[pl.BlockSpec((tr, n), lambda i: (i, 0))],
        out_specs=pl.BlockSpec((tr, n), lambda i: (i, 0)),
        out_shape=jax.ShapeDtypeStruct((r, n), BF16),
        compiler_params=_params("parallel"),
        name="to_bf16",
    )(w2)
    return out.reshape(w.shape)


def _rope_table_kernel(pos_ref, inv_ref, sgn_ref, cos_ref, sin_ref):
    ang = pos_ref[0].astype(F32) * inv_ref[...]
    c = jnp.cos(ang)
    s = jnp.sin(ang) * sgn_ref[...]
    cos_ref[0] = jnp.concatenate([c, c], axis=-1)
    sin_ref[0] = jnp.concatenate([s, s], axis=-1)


def _rope_tables(positions, ts):
    b, s = positions.shape
    half = HEAD_DIM // 2
    inv = ROPE_THETA ** (-jnp.arange(half, dtype=F32) / half)
    inv = jnp.tile(inv, LANES // half).reshape(1, LANES)
    sgn = jnp.tile(jnp.concatenate([-jnp.ones((half,), F32), jnp.ones((half,), F32)]), LANES // HEAD_DIM)
    return pl.pallas_call(
        _rope_table_kernel,
        grid=(b, s // ts),
        in_specs=[pl.BlockSpec((1, ts, 1), lambda i, j: (i, j, 0)),
                  pl.BlockSpec((1, LANES), lambda i, j: (0, 0)),
                  pl.BlockSpec((1, LANES), lambda i, j: (0, 0))],
        out_specs=[pl.BlockSpec((1, ts, MIX_W), lambda i, j: (i, j, 0))] * 2,
        out_shape=[jax.ShapeDtypeStruct((b, s, MIX_W), F32)] * 2,
        compiler_params=_params("parallel", "parallel"),
        name="rope_tables",
    )(positions.reshape(b, s, 1), inv, sgn.reshape(1, LANES))


def _rwkv_kernel(c_ref, mu_ref, w0_ref, w2_ref, a0_ref, a2_ref, kk_ref, ka_ref, rk_ref, lnw_ref, lnb_ref,
                 o_ref, carry_ref, st_ref, y_ref, *, ts, ch):
    @pl.when(pl.program_id(1) == 0)
    def _():
        carry_ref[...] = jnp.zeros_like(carry_ref)
        st_ref[...] = jnp.zeros_like(st_ref)

    c = c_ref[0]
    row = lax.broadcasted_iota(I32, c.shape, 0)
    prev = jnp.where(row == 0, carry_ref[0:1, :], pltpu.roll(c, 1, axis=0))
    carry_ref[0:1, :] = c[ts - 1:ts, :]
    c = c + mu_ref[...] * (prev - c)
    w = MIX_W
    r, k, v = c[:, 0:w], c[:, w:2 * w], c[:, 2 * w:3 * w]
    wl, al = c[:, 3 * w:3 * w + RWKV_LORA], c[:, 3 * w + RWKV_LORA:3 * w + 2 * RWKV_LORA]
    g = c[:, 3 * w + 2 * RWKV_LORA:]

    z = w0_ref[...] + _mm3(jnp.tanh(wl), w2_ref[...])
    w_log = jnp.minimum(z, 0.0) - jnp.log(1.0 + jnp.exp(-jnp.abs(z))) - 0.5
    logw = -jnp.exp(w_log)
    a = _sigmoid(a0_ref[...] + _mm3(al, a2_ref[...]))
    bd = _head_ones(w)
    kk = k * kk_ref[...]
    kk = kk / jnp.maximum(jnp.sqrt(_mm_ones_rhs(kk * kk, bd)), 1e-12)
    kmod = k * (1.0 + (a - 1.0) * ka_ref[...])
    kka = kk * a

    shift = int(math.log2(ch))
    ri = lax.broadcasted_iota(I32, (ts, ts), 0)
    ci = lax.broadcasted_iota(I32, (ts, ts), 1)
    ltri = jnp.where((ci <= ri) & ((ri >> shift) == (ci >> shift)), 1.0, 0.0).astype(BF16)
    cum = _mm_ones_lhs(ltri, logw)
    e_in = jnp.exp(cum)
    a_hat = -kk * jnp.exp(cum - logw)
    r_hat = r * e_in
    e_inv = jnp.exp(-cum)
    b_hat = kka * e_inv
    k_hat = kmod * e_inv

    ti = lax.broadcasted_iota(I32, (2 * ch, ch), 0)
    si = lax.broadcasted_iota(I32, (2 * ch, ch), 1)
    tri2 = si < jnp.where(ti < ch, ti, ti - ch + 1)
    eye = jnp.where(lax.broadcasted_iota(I32, (ch, ch), 0) == lax.broadcasted_iota(I32, (ch, ch), 1), 1.0, 0.0)

    n_ch = ts // ch
    pairs = [(ci_, h) for ci_ in range(n_ch) for h in range(N_HEADS)]
    blk = lambda t, ci_, h: t[ci_ * ch:(ci_ + 1) * ch, h * HEAD_DIM:(h + 1) * HEAD_DIM]
    ar = [_split(jnp.concatenate([blk(a_hat, *p), blk(r_hat, *p)], axis=0)) for p in pairs]
    vhs = [blk(v, *p) for p in pairs]
    x_b = [jnp.where(tri2, _mm3s(ar[i], _split(blk(b_hat, *p)), _NT), 0.0) for i, p in enumerate(pairs)]
    x_k = [jnp.where(tri2, _mm3s(ar[i], _split(blk(k_hat, *p)), _NT), 0.0) for i, p in enumerate(pairs)]
    pw = [x[:ch] for x in x_b]
    tinv = [eye + p for p in pw]
    for _ in range(shift - 1):
        pws = [_split(p) for p in pw]
        pw = [_mm3s(p, p) for p in pws]
        tinv = [_mm3(t, eye + p) for t, p in zip(tinv, pw)]
    from_v = [_mm(x, vh) for x, vh in zip(x_k, vhs)]

    st = [st_ref[h] for h in range(N_HEADS)]
    for ci_ in range(n_ch):
        lo, hi = ci_ * ch, (ci_ + 1) * ch
        cum_end = cum[hi - 1:hi, :]
        e_tail = jnp.exp(cum_end - cum[lo:hi, :])
        b_til = kka[lo:hi, :] * e_tail
        k_til = kmod[lo:hi, :] * e_tail
        p_end = jnp.exp(cum_end)
        ids = [ci_ * N_HEADS + h for h in range(N_HEADS)]
        hs = [slice(h * HEAD_DIM, (h + 1) * HEAD_DIM) for h in range(N_HEADS)]
        from_state = [_mm3s(ar[i], _split(st[h]), _NT) for h, i in enumerate(ids)]
        u = [_mm3(tinv[i], from_state[h][:ch] + from_v[i][:ch]) for h, i in enumerate(ids)]
        for h, i in enumerate(ids):
            y_ref[lo:hi, hs[h]] = from_state[h][ch:] + from_v[i][ch:] + _mm(x_b[i][ch:], u[h])
        st = [st[h] * p_end[:, hs[h]] + _mm3(jnp.concatenate([u[h], vhs[i]], axis=0),
                                             jnp.concatenate([b_til[:, hs[h]], k_til[:, hs[h]]], axis=0), _TN)
              for h, i in enumerate(ids)]
    for h in range(N_HEADS):
        st_ref[h] = st[h]

    yn = _head_norm(y_ref[...], bd, RWKV_LNX_EPS) * lnw_ref[...] + lnb_ref[...]
    bonus = _mm_ones_rhs(r * kmod * rk_ref[...], bd) * v
    o_ref[0] = (yn + bonus) * _silu(g)


def _rwkv(c, mu, w0, w2, a0, a2, k_k, k_a, r_k, lnx_w, lnx_b, ts, ch):
    b, s, n = c.shape
    row = lambda t: t.reshape(1, -1)
    vec = lambda m: pl.BlockSpec((1, m), lambda i, j: (0, 0))
    mat = lambda m: pl.BlockSpec((RWKV_LORA, m), lambda i, j: (0, 0))
    return pl.pallas_call(
        functools.partial(_rwkv_kernel, ts=ts, ch=ch),
        grid=(b, s // ts),
        in_specs=[pl.BlockSpec((1, ts, n), lambda i, j: (i, j, 0)),
                  vec(n), vec(MIX_W), mat(MIX_W), vec(MIX_W), mat(MIX_W), vec(MIX_W), vec(MIX_W), vec(MIX_W),
                  vec(MIX_W), vec(MIX_W)],
        out_specs=pl.BlockSpec((1, ts, MIX_W), lambda i, j: (i, j, 0)),
        out_shape=jax.ShapeDtypeStruct((b, s, MIX_W), F32),
        scratch_shapes=[pltpu.VMEM((8, n), F32),
                        pltpu.VMEM((N_HEADS, HEAD_DIM, HEAD_DIM), F32),
                        pltpu.VMEM((ts, MIX_W), F32)],
        compiler_params=_params("parallel", "arbitrary"),
        name="rwkv7",
    )(c, row(mu), row(w0), w2, row(a0), a2, row(k_k), row(k_a), row(r_k), row(lnx_w), row(lnx_b))


def _ret_tables(tc):
    log_g = np.log(1.0 - np.exp(np.linspace(math.log(1.0 / 32), math.log(1.0 / 512), N_HEADS)))
    j = np.arange(tc, dtype=np.float64)
    rel = j[:, None] - j[None, :]
    dmask = np.where(rel >= 0, np.exp(log_g[:, None, None] * np.maximum(rel, 0.0)), 0.0)
    qdec = np.repeat(np.exp(log_g[None, :] * (j[:, None] + 1.0)), HEAD_DIM, axis=1)
    kdec = np.repeat(np.exp(log_g[None, :] * (tc - 1.0 - j[:, None])), HEAD_DIM, axis=1)
    cdec = np.repeat(np.exp(log_g * tc), HEAD_DIM)[None, :]
    f = lambda t: jnp.asarray(t, F32)
    return f(dmask), f(qdec), f(kdec), f(cdec)


def _ret_kernel(c_ref, cos_ref, sin_ref, dm_ref, qd_ref, kd_ref, cd_ref, gn_ref, o_ref, r_ref, y_ref):
    @pl.when(pl.program_id(1) == 0)
    def _():
        r_ref[...] = jnp.zeros_like(r_ref)

    w = MIX_W
    c = c_ref[0]
    cos, sin = cos_ref[0], sin_ref[0]
    q = _rope(c[:, 0:w], cos, sin)
    k = _rope(c[:, w:2 * w], cos, sin) * (HEAD_DIM ** -0.5)
    v = c[:, 2 * w:3 * w]
    g = c[:, 3 * w:]
    kd = k * kd_ref[...]
    qd = qd_ref[...]
    cd = cd_ref[...]
    hs = range(N_HEADS)
    ls = [slice(h * HEAD_DIM, (h + 1) * HEAD_DIM) for h in hs]
    qh = [q[:, l].astype(BF16) for l in ls]
    vh = [v[:, l].astype(BF16) for l in ls]
    rs = [r_ref[h] for h in hs]
    att = [_mm(qh[h], k[:, ls[h]], _NT) * dm_ref[h] for h in hs]
    cross = [_mm(qh[h], rs[h]) * qd[:, ls[h]] for h in hs]
    for h in hs:
        y_ref[:, ls[h]] = _mm(att[h], vh[h]) + cross[h]
    for h in hs:
        r_ref[h] = rs[h] * cd[:, ls[h]] + _mm(kd[:, ls[h]], vh[h], _TN)
    bd = _head_ones(w)
    o_ref[0] = _head_norm(y_ref[...], bd, NORM_EPS) * gn_ref[...] * _silu(g)


def _retention(c, cos, sin, gn_w, tc):
    b, s, n = c.shape
    dmask, qdec, kdec, cdec = _ret_tables(tc)
    tok = lambda m: pl.BlockSpec((1, tc, m), lambda i, j: (i, j, 0))
    full = lambda shape: pl.BlockSpec(shape, lambda i, j: (0,) * len(shape))
    return pl.pallas_call(
        _ret_kernel,
        grid=(b, s // tc),
        in_specs=[tok(n), tok(MIX_W), tok(MIX_W), full((N_HEADS, tc, tc)), full((tc, MIX_W)), full((tc, MIX_W)),
                  full((1, MIX_W)), full((1, MIX_W))],
        out_specs=tok(MIX_W),
        out_shape=jax.ShapeDtypeStruct((b, s, MIX_W), F32),
        scratch_shapes=[pltpu.VMEM((N_HEADS, HEAD_DIM, HEAD_DIM), F32), pltpu.VMEM((tc, MIX_W), F32)],
        compiler_params=_params("parallel", "arbitrary"),
        name="retention",
    )(c, cos, sin, dmask, qdec, kdec, cdec, gn_w.reshape(1, MIX_W))


def _s5_tables(lam_re, lam_im, log_dt, b_re, b_im, c_re, c_im, tl):
    gn, p = lam_re.shape
    lr = jnp.minimum(lam_re.astype(F32), -1e-4)
    li = lam_im.astype(F32)
    dt = jnp.exp(log_dt.astype(F32))[:, None]
    mag = jnp.exp(lr * dt)
    ab_re, ab_im = mag * jnp.cos(li * dt), mag * jnp.sin(li * dt)
    den = lr * lr + li * li
    f_re = ((ab_re - 1.0) * lr + ab_im * li) / den
    f_im = (ab_im * lr - (ab_re - 1.0) * li) / den
    bb_re = f_re[..., None] * b_re - f_im[..., None] * b_im
    bb_im = f_re[..., None] * b_im + f_im[..., None] * b_re
    eye = jnp.eye(gn, dtype=F32)
    bd_in = lambda bb: jnp.einsum('gpc,gh->gchp', bb, eye).reshape(gn * S5_GROUP, gn * p)
    w_in = jnp.concatenate([bd_in(bb_re), bd_in(bb_im)], axis=1)
    bd_out = lambda cc: jnp.einsum('gcp,gh->gphc', cc, eye).reshape(gn * p, gn * S5_GROUP)
    w_out = jnp.concatenate([bd_out(c_re), -bd_out(c_im)], axis=0)
    def powers(n):
        n = n.astype(F32)[:, None, None]
        m = jnp.exp(n * (lr * dt)[None])
        ang = n * (li * dt)[None]
        return (m * jnp.cos(ang)).reshape(-1, gn * p), (m * jnp.sin(ang)).reshape(-1, gn * p)
    steps = 2 ** jnp.arange(int(math.log2(tl)))
    dbl_re, dbl_im = powers(steps)
    car_re, car_im = powers(jnp.arange(1, tl + 1))
    return w_in, w_out, dbl_re, dbl_im, car_re, car_im


def _s5_kernel(c_ref, win_ref, wout_ref, dr_ref, di_ref, cr_ref, ci_ref, dskip_ref, wglu_ref, o_ref,
               xr_ref, xi_ref, *, tl):
    @pl.when(pl.program_id(1) == 0)
    def _():
        xr_ref[...] = jnp.zeros_like(xr_ref)
        xi_ref[...] = jnp.zeros_like(xi_ref)

    ns = S5_GROUPS * S5_STATE
    c = c_ref[0]
    u, g = c[:, :MIX_W], c[:, MIX_W:]
    bu = _mm(u, win_ref[...])
    xr, xi = bu[:, :ns], bu[:, ns:]
    row = lax.broadcasted_iota(I32, (tl, ns), 0)
    for j in range(int(math.log2(tl))):
        d = 1 << j
        ar, ai = dr_ref[j:j + 1, :], di_ref[j:j + 1, :]
        sr = jnp.where(row >= d, pltpu.roll(xr, d, axis=0), 0.0)
        si = jnp.where(row >= d, pltpu.roll(xi, d, axis=0), 0.0)
        xr, xi = xr + ar * sr - ai * si, xi + ar * si + ai * sr
    pr, pi = cr_ref[...], ci_ref[...]
    x0r, x0i = xr_ref[0:1, :], xi_ref[0:1, :]
    xr, xi = xr + pr * x0r - pi * x0i, xi + pr * x0i + pi * x0r
    xr_ref[0:1, :] = xr[tl - 1:tl, :]
    xi_ref[0:1, :] = xi[tl - 1:tl, :]
    y = _mm(jnp.concatenate([xr, xi], axis=-1), wout_ref[...]) + dskip_ref[...] * u
    y = y * (0.5 * (1.0 + jnp.tanh(math.sqrt(2.0 / math.pi) * (y + 0.044715 * (y * y * y)))))
    y = y * _sigmoid(_mm(y, wglu_ref[...]))
    o_ref[0] = y * _silu(g)


def _s5(c, lam_re, lam_im, log_dt, b_re, b_im, c_re, c_im, d_skip, w_glu, tl):
    b, s, n = c.shape
    ns = S5_GROUPS * S5_STATE
    w_in, w_out, dbl_re, dbl_im, car_re, car_im = _s5_tables(lam_re, lam_im, log_dt, b_re, b_im, c_re, c_im, tl)
    nd = dbl_re.shape[0]
    tok = lambda m: pl.BlockSpec((1, tl, m), lambda i, j: (i, j, 0))
    full = lambda shape: pl.BlockSpec(shape, lambda i, j: (0,) * len(shape))
    return pl.pallas_call(
        functools.partial(_s5_kernel, tl=tl),
        grid=(b, s // tl),
        in_specs=[tok(n), full((MIX_W, 2 * ns)), full((2 * ns, MIX_W)), full((nd, ns)), full((nd, ns)),
                  full((tl, ns)), full((tl, ns)), full((1, MIX_W)), full((MIX_W, MIX_W))],
        out_specs=tok(MIX_W),
        out_shape=jax.ShapeDtypeStruct((b, s, MIX_W), F32),
        scratch_shapes=[pltpu.VMEM((8, ns), F32), pltpu.VMEM((8, ns), F32)],
        compiler_params=_params("parallel", "arbitrary"),
        name="s5",
    )(c, w_in.astype(BF16), w_out.astype(BF16), dbl_re, dbl_im, car_re, car_im, d_skip.reshape(1, MIX_W),
      w_glu.astype(BF16))


def _xatt_kernel(c_ref, kv_ref, o_ref, y_ref):
    w = MIX_W
    c = c_ref[0]
    q, g = c[:, :w], c[:, w:]
    kv = kv_ref[0]
    for h in range(N_HEADS):
        ls = slice(h * HEAD_DIM, (h + 1) * HEAD_DIM)
        lg = _mm(q[:, ls], kv[:, ls], _NT) * (HEAD_DIM ** -0.5)
        p = jnp.exp(lg - jnp.max(lg, axis=-1, keepdims=True))
        y_ref[:, ls] = _mm(p, kv[:, w + h * HEAD_DIM:w + (h + 1) * HEAD_DIM]) / jnp.sum(p, axis=-1, keepdims=True)
    o_ref[0] = y_ref[...] * _silu(g)


def _xatt(c, mem_kv, tq):
    b, s, n = c.shape
    nm = mem_kv.shape[1]
    return pl.pallas_call(
        _xatt_kernel,
        grid=(b, s // tq),
        in_specs=[pl.BlockSpec((1, tq, n), lambda i, j: (i, j, 0)),
                  pl.BlockSpec((1, nm, 2 * MIX_W), lambda i, j: (i, 0, 0))],
        out_specs=pl.BlockSpec((1, tq, MIX_W), lambda i, j: (i, j, 0)),
        out_shape=jax.ShapeDtypeStruct((b, s, MIX_W), F32),
        scratch_shapes=[pltpu.VMEM((tq, MIX_W), F32)],
        compiler_params=_params("parallel", "parallel"),
        name="mem_xatt",
    )(c, mem_kv)


def _dsa_prep_kernel(c_ref, ikw_ref, cos_ref, sin_ref, q_ref, k_ref, v_ref, iq_ref, ik_ref):
    w = MIX_W
    c = c_ref[0]
    cos, sin = cos_ref[0], sin_ref[0]
    q = _rope(c[:, 0:w], cos, sin) * (HEAD_DIM ** -0.5)
    k = _rope(c[:, w:2 * w], cos, sin)
    v = c[:, 2 * w:3 * w]
    iq = _rope(c[:, 3 * w:4 * w], cos, sin) * (IDX_DIM ** -0.5)
    ik = _rope(ikw_ref[0], cos[:, :LANES], sin[:, :LANES])
    one_col = jnp.where(lax.broadcasted_iota(I32, (c.shape[0], HEAD_DIM), 1) == 0, 1.0, 0.0)
    for h in range(N_HEADS):
        ls = slice(h * HEAD_DIM, (h + 1) * HEAD_DIM)
        q_ref[0, h] = q[:, ls].astype(BF16)
        k_ref[0, h] = k[:, ls].astype(BF16)
        v_ref[0, h] = jnp.concatenate([v[:, ls], one_col], axis=-1).astype(BF16)
        iq_ref[0, h] = iq[:, ls].astype(BF16)
    ik_ref[0] = ik[:, :IDX_DIM].astype(BF16)


def _dsa_prep(c_main, c_ikw, cos, sin, tp):
    b, s, _ = c_main.shape
    tok = lambda m: pl.BlockSpec((1, tp, m), lambda i, j: (i, j, 0))
    hm = lambda m: pl.BlockSpec((1, N_HEADS, tp, m), lambda i, j: (i, 0, j, 0))
    hm_shape = lambda m: jax.ShapeDtypeStruct((b, N_HEADS, s, m), BF16)
    return pl.pallas_call(
        _dsa_prep_kernel,
        grid=(b, s // tp),
        in_specs=[pl.BlockSpec((1, tp, 4 * MIX_W), lambda i, j: (i, j, 0)), tok(LANES), tok(MIX_W), tok(MIX_W)],
        out_specs=[hm(HEAD_DIM), hm(HEAD_DIM), hm(2 * HEAD_DIM), hm(HEAD_DIM), tok(IDX_DIM)],
        out_shape=[hm_shape(HEAD_DIM), hm_shape(HEAD_DIM), hm_shape(2 * HEAD_DIM), hm_shape(HEAD_DIM),
                   jax.ShapeDtypeStruct((b, s, IDX_DIM), BF16)],
        compiler_params=_params("parallel", "parallel"),
        name="dsa_prep",
    )(c_main, c_ikw, cos, sin)


def _dsa_kernel(q_ref, iq_ref, ikw_ref, g_ref, k_ref, v_ref, ik_ref, tri_ref, o_ref, key_ref, planes_ref, m_ref,
                acc_ref, *, qb, kc, n_sel, n_tile):
    q0 = pl.program_id(1) * qb
    nkc = (q0 + qb + kc - 1) // kc
    qidx = q0 + lax.broadcasted_iota(I32, (qb, kc), 0)
    lane = lax.broadcasted_iota(I32, (qb, kc), 1)
    wide = lambda t: jnp.concatenate([t] * (kc // LANES), axis=1)
    ones_k = jnp.ones((kc, LANES), BF16)
    chunk = lambda c: pl.ds(pl.multiple_of(c * kc, kc), kc)
    iw = ikw_ref[0] * (IDX_HEADS ** -0.5)
    iws = [iw[:, IDX_DIM + h:IDX_DIM + h + 1] for h in range(IDX_HEADS)]

    def score_body(c, carry):
        ikc = ik_ref[0, chunk(c), :]
        s = [lax.dot_general(iq_ref[0, h], ikc, _NT, preferred_element_type=F32) for h in range(IDX_HEADS)]
        acc = jnp.maximum(s[0], 0.0) * iws[0]
        for h in range(1, IDX_HEADS):
            acc = acc + jnp.maximum(s[h], 0.0) * iws[h]
        acc = jnp.where(acc == 0.0, 0.0, acc)
        bits = pltpu.bitcast(acc, I32)
        key = jnp.where(bits < 0, bits ^ 0x7FFFFFFF, bits)
        key_ref[:, chunk(c)] = jnp.where(c * kc + lane <= qidx, key, INT_MIN)
        return carry

    lax.fori_loop(0, nkc, score_body, 0)

    def fill_body(c, carry):
        key_ref[:, chunk(c)] = jnp.full((qb, kc), INT_MIN, I32)
        return carry

    lax.fori_loop(nkc, n_tile * LANES // kc, fill_body, 0)

    def plane_body(gidx, carry):
        rows = pl.ds(pl.multiple_of(gidx * 8, 8), 8)
        tiles = [key_ref[rows, i * LANES:(i + 1) * LANES] ^ INT_MIN for i in range(n_tile)]
        tiles += [jnp.zeros((8, LANES), I32)] * (32 - n_tile)
        for p, plane in enumerate(_bit_transpose32(tiles)):
            planes_ref[p, rows, :] = plane
        return carry

    lax.fori_loop(0, qb // 8, plane_body, 0)

    ones_l = jnp.ones((LANES, LANES), BF16)

    def bit_body(t, carry):
        alive, above, vbits = carry
        p = 2 * t
        hi, lo = planes_ref[p], planes_ref[p + 1]
        a1 = alive & hi
        a0 = alive ^ a1
        o11 = a1 & lo
        o10 = a1 ^ o11
        o01 = a0 & lo
        pc = jnp.concatenate([lax.population_count(o).astype(F32).astype(BF16) for o in (o11, o10, o01)], axis=0)
        cnt = jnp.dot(pc, ones_l, preferred_element_type=F32)
        s1 = above + cnt[:qb]
        s2 = s1 + cnt[qb:2 * qb]
        s3 = s2 + cnt[2 * qb:]
        t11, t10, t01 = s1 >= n_sel, s2 >= n_sel, s3 >= n_sel
        alive = jnp.where(t11, o11, jnp.where(t10, o10, jnp.where(t01, o01, a0 ^ o01)))
        above = jnp.where(t11, above, jnp.where(t10, s1, jnp.where(t01, s2, s3)))
        digit = jnp.where(t11, 3, jnp.where(t10, 2, jnp.where(t01, 1, 0)))
        return alive, above, vbits | jnp.left_shift(digit, 30 - p)

    _, above, vbits = lax.fori_loop(
        0, 16, bit_body, (jnp.full((qb, LANES), -1, I32), jnp.zeros((qb, LANES), F32), jnp.zeros((qb, LANES), I32)))
    vs = wide(jnp.maximum(vbits ^ INT_MIN, INT_MIN + 1))
    need = wide(n_sel - above)

    def bias_body(c, run):
        kv = key_ref[:, chunk(c)]
        tied = kv == vs
        tied_b = jnp.where(tied, 1.0, 0.0).astype(BF16)
        rank = wide(run) + jnp.dot(tied_b, tri_ref[...], preferred_element_type=F32)
        bias = jnp.where(kv > vs, 0.0, jnp.where(tied, jnp.where(rank <= need, 0.0, -1e30), -1e30))
        key_ref[:, chunk(c)] = pltpu.bitcast(bias, I32)
        return run + jnp.dot(tied_b, ones_k, preferred_element_type=F32)

    lax.fori_loop(0, nkc, bias_body, jnp.zeros((qb, LANES), F32))

    m_ref[...] = jnp.full(m_ref.shape, -1e30, F32)
    acc_ref[...] = jnp.zeros(acc_ref.shape, F32)

    def att_body(c, carry):
        bias = pltpu.bitcast(key_ref[:, chunk(c)], F32)
        hs = range(N_HEADS)
        lg = [lax.dot_general(q_ref[0, h], k_ref[0, h, chunk(c), :], _NT, preferred_element_type=F32) + bias
              for h in hs]
        m_old = [m_ref[h] for h in hs]
        m_new = [jnp.maximum(m_old[h], jnp.broadcast_to(jnp.max(lg[h], axis=-1, keepdims=True), (qb, LANES)))
                 for h in hs]
        p = [jnp.exp(lg[h] - wide(m_new[h])).astype(BF16) for h in hs]
        for h in hs:
            acc_ref[h] = jnp.exp(m_old[h] - m_new[h]) * acc_ref[h] + jnp.dot(p[h], v_ref[0, h, chunk(c), :],
                                                                             preferred_element_type=F32)
            m_ref[h] = m_new[h]
        return carry

    lax.fori_loop(0, nkc, att_body, 0)
    y = jnp.concatenate([acc_ref[h][:, :HEAD_DIM] / acc_ref[h][:, HEAD_DIM:HEAD_DIM + 1] for h in range(N_HEADS)],
                        axis=-1)
    o_ref[0] = y * _silu(g_ref[0])


def _dsa(c_main, c_ikw, cos, sin, qb, kc, tp):
    b, s, _ = c_main.shape
    q, k, v, iq, ik = _dsa_prep(c_main, c_ikw, cos, sin, tp)
    n_sel = min(TOPK_MAX, s // 4)
    n_tile = s // LANES
    assert n_tile <= 32 and s % kc == 0, "one key per bit of a 32-bit plane word"
    tri = jnp.asarray(np.triu(np.ones((kc, kc), np.float32)), BF16)
    qblk = pl.BlockSpec((1, N_HEADS, qb, HEAD_DIM), lambda i, j: (i, 0, j, 0))
    return pl.pallas_call(
        functools.partial(_dsa_kernel, qb=qb, kc=kc, n_sel=n_sel, n_tile=n_tile),
        grid=(b, s // qb),
        in_specs=[qblk, qblk,
                  pl.BlockSpec((1, qb, LANES), lambda i, j: (i, j, 0)),
                  pl.BlockSpec((1, qb, MIX_W), lambda i, j: (i, j, 4)),
                  pl.BlockSpec((1, N_HEADS, s, HEAD_DIM), lambda i, j: (i, 0, 0, 0)),
                  pl.BlockSpec((1, N_HEADS, s, 2 * HEAD_DIM), lambda i, j: (i, 0, 0, 0)),
                  pl.BlockSpec((1, s, IDX_DIM), lambda i, j: (i, 0, 0)),
                  pl.BlockSpec((kc, kc), lambda i, j: (0, 0))],
        out_specs=pl.BlockSpec((1, qb, MIX_W), lambda i, j: (i, j, 0)),
        out_shape=jax.ShapeDtypeStruct((b, s, MIX_W), F32),
        scratch_shapes=[pltpu.VMEM((qb, s), I32), pltpu.VMEM((32, qb, LANES), I32),
                        pltpu.VMEM((N_HEADS, qb, LANES), F32),
                        pltpu.VMEM((N_HEADS, qb, LANES), F32)],
        compiler_params=_params("parallel", "arbitrary"),
        name="dsa",
    )(q, iq, c_ikw, c_main, k, v, ik, tri)


def _merge_kernel(x_ref, gpre_ref, gpost_ref, wg_ref, wb_ref, wo_ref, y0, y1, y2, y3, y4, o_ref):
    x = x_ref[...]
    hb = _rms(x, gpre_ref[...]).astype(BF16)
    merged = None
    for i, y_ref in enumerate((y0, y1, y2, y3, y4)):
        gate = _sigmoid(jnp.dot(hb, wg_ref[:, i * D_MODEL:(i + 1) * D_MODEL], preferred_element_type=F32))
        term = gate * jnp.dot(y_ref[...].astype(BF16), wb_ref[i], preferred_element_type=F32)
        merged = term if merged is None else merged + term
    out = jnp.dot(merged.astype(BF16), wo_ref[...], preferred_element_type=F32)
    o_ref[...] = x + _rms(out, gpost_ref[...])


def _merge(x2d, g_pre, g_post, w_gate, w_branch, w_out, ys, tm):
    t, d = x2d.shape
    tok = lambda m: pl.BlockSpec((tm, m), lambda i: (i, 0))
    full = lambda shape: pl.BlockSpec(shape, lambda i: (0,) * len(shape))
    return pl.pallas_call(
        _merge_kernel,
        grid=(t // tm,),
        in_specs=[tok(d), full((1, d)), full((1, d)), full((d, GATE_COLS)), full((N_BRANCH, MIX_W, d)),
                  full((d, d))] + [tok(MIX_W)] * N_BRANCH,
        out_specs=tok(d),
        out_shape=jax.ShapeDtypeStruct((t, d), F32),
        compiler_params=_params("parallel"),
        name="merge",
    )(x2d, g_pre.reshape(1, d), g_post.reshape(1, d), w_gate, w_branch, w_out, *ys)


def kernel(x, mem, positions, norm_pre, norm_post, norm_mem, w_in, rwkv_mu, rwkv_w0, rwkv_w2, rwkv_a0, rwkv_a2, rwkv_k_k, rwkv_k_a, rwkv_r_k, rwkv_lnx_w, rwkv_lnx_b, ret_gn_w, s5_lam_re, s5_lam_im, s5_log_dt, s5_b_re, s5_b_im, s5_c_re, s5_c_im, s5_d, s5_w_glu, w_mem_kv, w_branch, w_out):
    b, s, d = x.shape
    t = b * s
    n_mem = mem.shape[1]
    depth = w_in.shape[0]
    tm = min(256, t)
    cos, sin = _rope_tables(positions, min(512, s))
    o_dsa = RWKV_COLS
    o_ret = o_dsa + DSA_COLS
    o_s5 = o_ret + RET_COLS
    o_x = o_s5 + S5_COLS
    o_gate = o_x + XATT_COLS
    widths = (RWKV_COLS, 5 * MIX_W, LANES, RET_COLS, S5_COLS, XATT_COLS)
    for l in range(depth):
        w = w_in[l]
        o_idx = o_dsa + 4 * MIX_W
        o_g = o_idx + IDX_DIM + IDX_HEADS
        w_cat = _to_bf16(jnp.concatenate([
            w[:, :RWKV_COLS],
            w[:, o_dsa:o_idx], w[:, o_g:o_ret],
            w[:, o_idx:o_g], jnp.zeros((d, LANES - IDX_DIM - IDX_HEADS), w.dtype),
            w[:, o_ret:o_gate]], axis=1))
        x2 = x.reshape(t, d)
        c_rwkv, c_dsa, c_ikw, c_ret, c_s5, c_x = _norm_proj(x2, norm_pre[l], w_cat, widths, tm)
        (mem_kv,) = _norm_proj(mem.reshape(b * n_mem, d), norm_mem[l], _to_bf16(w_mem_kv[l]), (2 * MIX_W,),
                               min(256, b * n_mem))
        r3 = lambda a: a.reshape(b, s, a.shape[-1])
        y_rwkv = _rwkv(r3(c_rwkv), rwkv_mu[l], rwkv_w0[l], rwkv_w2[l], rwkv_a0[l], rwkv_a2[l], rwkv_k_k[l],
                       rwkv_k_a[l], rwkv_r_k[l], rwkv_lnx_w[l], rwkv_lnx_b[l], ts=min(256, s), ch=64)
        y_dsa = _dsa(r3(c_dsa), r3(c_ikw), cos, sin, qb=min(512, s), kc=min(512, s), tp=min(512, s))
        y_ret = _retention(r3(c_ret), cos, sin, ret_gn_w[l], tc=min(256, s))
        y_s5 = _s5(r3(c_s5), s5_lam_re[l], s5_lam_im[l], s5_log_dt[l], s5_b_re[l], s5_b_im[l], s5_c_re[l],
                   s5_c_im[l], s5_d[l], s5_w_glu[l], tl=min(128, s))
        y_x = _xatt(r3(c_x), mem_kv.reshape(b, n_mem, 2 * MIX_W), tq=min(512, s))
        ys = [y.reshape(t, MIX_W) for y in (y_rwkv, y_dsa, y_ret, y_s5, y_x)]
        x = _merge(x2, norm_pre[l], norm_post[l], _to_bf16(w[:, o_gate:]), _to_bf16(w_branch[l]),
                   _to_bf16(w_out[l]), ys, tm).reshape(b, s, d)
    return x
```

```python
import functools
import math

import numpy as np
import jax
import jax.numpy as jnp
from jax import lax
from jax.experimental import pallas as pl
from jax.experimental.pallas import tpu as pltpu

F32, BF16, I32 = jnp.float32, jnp.bfloat16, jnp.int32
HIGHEST = lax.Precision.HIGHEST

D_MODEL = 1024
HEAD_DIM = 64
N_HEADS = 4
MIX_W = 256
RWKV_LORA = 64
RWKV_COLS = 4 * MIX_W + 2 * RWKV_LORA
RWKV_LNX_EPS = 64e-5
IDX_HEADS = 4
IDX_DIM = 64
TOPK_MAX = 256
DSA_COLS = 4 * MIX_W + IDX_HEADS * IDX_DIM + IDX_DIM + IDX_HEADS
RET_COLS = 4 * MIX_W
S5_GROUP = 16
S5_GROUPS = MIX_W // S5_GROUP
S5_STATE = 64
S5_COLS = 2 * MIX_W
XATT_COLS = 2 * MIX_W
N_BRANCH = 5
GATE_COLS = N_BRANCH * D_MODEL
ROPE_THETA = 10000.0
NORM_EPS = 1e-6

LANES = 128
VMEM_LIMIT = 48 * 1024 * 1024
INT_MIN = -(2 ** 31)

_NN = (((1,), (0,)), ((), ()))
_NT = (((1,), (1,)), ((), ()))
_TN = (((0,), (0,)), ((), ()))


def _mm(a, b, dims=_NN):
    return lax.dot_general(a.astype(BF16), b.astype(BF16), dims, preferred_element_type=F32)


def _split(a):
    hi = a.astype(BF16)
    return hi, (a - hi.astype(F32)).astype(BF16)


def _mm3s(a, b, dims=_NN):
    d = lambda x, y: lax.dot_general(x, y, dims, preferred_element_type=F32)
    return d(a[0], b[0]) + (d(a[0], b[1]) + d(a[1], b[0]))


def _mm3(a, b, dims=_NN):
    return _mm3s(_split(a), _split(b), dims)


def _split3(x):
    hi = x.astype(BF16)
    r1 = x - hi.astype(F32)
    mid = r1.astype(BF16)
    return hi, mid, (r1 - mid.astype(F32)).astype(BF16)


def _mm_ones_rhs(x, ones_bf16):
    d = lambda p: jnp.dot(p, ones_bf16, preferred_element_type=F32)
    hi, mid, lo = _split3(x)
    return d(hi) + (d(mid) + d(lo))


def _mm_ones_lhs(ones_bf16, x):
    d = lambda p: jnp.dot(ones_bf16, p, preferred_element_type=F32)
    hi, mid, lo = _split3(x)
    return d(hi) + (d(mid) + d(lo))


def _sigmoid(x):
    return 1.0 / (1.0 + jnp.exp(-x))


def _silu(x):
    return x * _sigmoid(x)


def _rms(x, g):
    return x * lax.rsqrt(jnp.mean(x * x, axis=-1, keepdims=True) + NORM_EPS) * g


def _head_ones(n):
    r = lax.broadcasted_iota(I32, (n, n), 0) >> 6
    c = lax.broadcasted_iota(I32, (n, n), 1) >> 6
    return jnp.where(r == c, 1.0, 0.0).astype(BF16)


def _head_norm(y, bd, eps):
    mu = _mm_ones_rhs(y, bd) * (1.0 / HEAD_DIM)
    d = y - mu
    var = _mm_ones_rhs(d * d, bd) * (1.0 / HEAD_DIM)
    return d * lax.rsqrt(var + eps)


def _bit_transpose32(a):
    a = list(a)
    j, m = 16, 0x0000FFFF
    while j:
        k = 0
        while k < 32:
            t = (a[k] ^ lax.shift_right_logical(a[k + j], jnp.full_like(a[k + j], j))) & m
            a[k] = a[k] ^ t
            a[k + j] = a[k + j] ^ (t << j)
            k = (k + j + 1) & ~j
        j >>= 1
        m = m ^ (m << j)
    return a


def _rope(x, cos, sin_signed):
    n = x.shape[-1]
    lane = lax.broadcasted_iota(I32, x.shape, x.ndim - 1)
    first_half = (lane & (HEAD_DIM - 1)) < HEAD_DIM // 2
    partner = jnp.where(first_half, pltpu.roll(x, n - HEAD_DIM // 2, axis=x.ndim - 1),
                        pltpu.roll(x, HEAD_DIM // 2, axis=x.ndim - 1))
    return x * cos + partner * sin_signed


def _params(*sem):
    return pltpu.CompilerParams(dimension_semantics=sem, vmem_limit_bytes=VMEM_LIMIT)


def _norm_proj_kernel(x_ref, g_ref, w_ref, *o_refs, widths):
    hb = _rms(x_ref[...], g_ref[...]).astype(BF16)
    off = 0
    for o_ref, n in zip(o_refs, widths):
        o_ref[...] = jnp.dot(hb, w_ref[:, off:off + n], preferred_element_type=F32)
        off += n


def _norm_proj(x2d, g, w_bf16, widths, tm):
    t, d = x2d.shape
    n = sum(widths)
    return pl.pallas_call(
        functools.partial(_norm_proj_kernel, widths=widths),
        grid=(t // tm,),
        in_specs=[pl.BlockSpec((tm, d), lambda i: (i, 0)),
                  pl.BlockSpec((1, d), lambda i: (0, 0)),
                  pl.BlockSpec((d, n), lambda i: (0, 0), pipeline_mode=pl.Buffered(1))],
        out_specs=[pl.BlockSpec((tm, w), lambda i: (i, 0)) for w in widths],
        out_shape=[jax.ShapeDtypeStruct((t, w), F32) for w in widths],
        compiler_params=_params("parallel"),
        name="norm_proj",
    )(x2d, g.reshape(1, d), w_bf16)


def _cast_kernel(x_ref, o_ref):
    o_ref[...] = x_ref[...].astype(o_ref.dtype)


def _to_bf16(w):
    w2 = w.reshape(-1, w.shape[-1])
    r, n = w2.shape
    tr = min(256, r)
    out = pl.pallas_call(
        _cast_kernel,
        grid=(r // tr,),
        in_specs=[pl.BlockSpec((tr, n), lambda i: (i, 0))],
        out_specs=pl.BlockSpec((tr, n), lambda i: (i, 0)),
        out_shape=jax.ShapeDtypeStruct((r, n), BF16),
        compiler_params=_params("parallel"),
        name="to_bf16",
    )(w2)
    return out.reshape(w.shape)


def _rope_table_kernel(pos_ref, inv_ref, sgn_ref, cos_ref, sin_ref):
    ang = pos_ref[0].astype(F32) * inv_ref[...]
    c = jnp.cos(ang)
    s = jnp.sin(ang) * sgn_ref[...]
    cos_ref[0] = jnp.concatenate([c, c], axis=-1)
    sin_ref[0] = jnp.concatenate([s, s], axis=-1)


def _rope_tables(positions, ts):
    b, s = positions.shape
    half = HEAD_DIM // 2
    inv = ROPE_THETA ** (-jnp.arange(half, dtype=F32) / half)
    inv = jnp.tile(inv, LANES // half).reshape(1, LANES)
    sgn = jnp.tile(jnp.concatenate([-jnp.ones((half,), F32), jnp.ones((half,), F32)]), LANES // HEAD_DIM)
    return pl.pallas_call(
        _rope_table_kernel,
        grid=(b, s // ts),
        in_specs=[pl.BlockSpec((1, ts, 1), lambda i, j: (i, j, 0)),
                  pl.BlockSpec((1, LANES), lambda i, j: (0, 0)),
                  pl.BlockSpec((1, LANES), lambda i, j: (0, 0))],
        out_specs=[pl.BlockSpec((1, ts, MIX_W), lambda i, j: (i, j, 0))] * 2,
        out_shape=[jax.ShapeDtypeStruct((b, s, MIX_W), F32)] * 2,
        compiler_params=_params("parallel", "parallel"),
        name="rope_tables",
    )(positions.reshape(b, s, 1), inv, sgn.reshape(1, LANES))


def _rwkv_kernel(c_ref, mu_ref, w0_ref, w2_ref, a0_ref, a2_ref, kk_ref, ka_ref, rk_ref, lnw_ref, lnb_ref,
                 o_ref, carry_ref, st_ref, y_ref, *, ts, ch):
    @pl.when(pl.program_id(1) == 0)
    def _():
        carry_ref[...] = jnp.zeros_like(carry_ref)
        st_ref[...] = jnp.zeros_like(st_ref)

    c = c_ref[0]
    row = lax.broadcasted_iota(I32, c.shape, 0)
    prev = jnp.where(row == 0, carry_ref[0:1, :], pltpu.roll(c, 1, axis=0))
    carry_ref[0:1, :] = c[ts - 1:ts, :]
    c = c + mu_ref[...] * (prev - c)
    w = MIX_W
    r, k, v = c[:, 0:w], c[:, w:2 * w], c[:, 2 * w:3 * w]
    wl, al = c[:, 3 * w:3 * w + RWKV_LORA], c[:, 3 * w + RWKV_LORA:3 * w + 2 * RWKV_LORA]
    g = c[:, 3 * w + 2 * RWKV_LORA:]

    z = w0_ref[...] + _mm3(jnp.tanh(wl), w2_ref[...])
    w_log = jnp.minimum(z, 0.0) - jnp.log(1.0 + jnp.exp(-jnp.abs(z))) - 0.5
    logw = -jnp.exp(w_log)
    a = _sigmoid(a0_ref[...] + _mm3(al, a2_ref[...]))
    bd = _head_ones(w)
    kk = k * kk_ref[...]
    kk = kk / jnp.maximum(jnp.sqrt(_mm_ones_rhs(kk * kk, bd)), 1e-12)
    kmod = k * (1.0 + (a - 1.0) * ka_ref[...])
    kka = kk * a

    shift = int(math.log2(ch))
    ri = lax.broadcasted_iota(I32, (ts, ts), 0)
    ci = lax.broadcasted_iota(I32, (ts, ts), 1)
    ltri = jnp.where((ci <= ri) & ((ri >> shift) == (ci >> shift)), 1.0, 0.0).astype(BF16)
    cum = _mm_ones_lhs(ltri, logw)
    e_in = jnp.exp(cum)
    a_hat = -kk * jnp.exp(cum - logw)
    r_hat = r * e_in
    e_inv = jnp.exp(-cum)
    b_hat = kka * e_inv
    k_hat = kmod * e_inv

    ti = lax.broadcasted_iota(I32, (2 * ch, ch), 0)
    si = lax.broadcasted_iota(I32, (2 * ch, ch), 1)
    tri2 = si < jnp.where(ti < ch, ti, ti - ch + 1)
    eye = jnp.where(lax.broadcasted_iota(I32, (ch, ch), 0) == lax.broadcasted_iota(I32, (ch, ch), 1), 1.0, 0.0)

    n_ch = ts // ch
    pairs = [(ci_, h) for ci_ in range(n_ch) for h in range(N_HEADS)]
    blk = lambda t, ci_, h: t[ci_ * ch:(ci_ + 1) * ch, h * HEAD_DIM:(h + 1) * HEAD_DIM]
    ar = [_split(jnp.concatenate([blk(a_hat, *p), blk(r_hat, *p)], axis=0)) for p in pairs]
    vhs = [blk(v, *p) for p in pairs]
    x_b = [jnp.where(tri2, _mm3s(ar[i], _split(blk(b_hat, *p)), _NT), 0.0) for i, p in enumerate(pairs)]
    x_k = [jnp.where(tri2, _mm3s(ar[i], _split(blk(k_hat, *p)), _NT), 0.0) for i, p in enumerate(pairs)]
    pw = [x[:ch] for x in x_b]
    tinv = [eye + p for p in pw]
    for _ in range(shift - 1):
        pws = [_split(p) for p in pw]
        pw = [_mm3s(p, p) for p in pws]
        tinv = [_mm3(t, eye + p) for t, p in zip(tinv, pw)]
    from_v = [_mm(x, vh) for x, vh in zip(x_k, vhs)]

    st = [st_ref[h] for h in range(N_HEADS)]
    for ci_ in range(n_ch):
        lo, hi = ci_ * ch, (ci_ + 1) * ch
        cum_end = cum[hi - 1:hi, :]
        e_tail = jnp.exp(cum_end - cum[lo:hi, :])
        b_til = kka[lo:hi, :] * e_tail
        k_til = kmod[lo:hi, :] * e_tail
        p_end = jnp.exp(cum_end)
        ids = [ci_ * N_HEADS + h for h in range(N_HEADS)]
        hs = [slice(h * HEAD_DIM, (h + 1) * HEAD_DIM) for h in range(N_HEADS)]
        from_state = [_mm3s(ar[i], _split(st[h]), _NT) for h, i in enumerate(ids)]
        u = [_mm3(tinv[i], from_state[h][:ch] + from_v[i][:ch]) for h, i in enumerate(ids)]
        for h, i in enumerate(ids):
            y_ref[lo:hi, hs[h]] = from_state[h][ch:] + from_v[i][ch:] + _mm(x_b[i][ch:], u[h])
        st = [st[h] * p_end[:, hs[h]] + _mm3(jnp.concatenate([u[h], vhs[i]], axis=0),
                                             jnp.concatenate([b_til[:, hs[h]], k_til[:, hs[h]]], axis=0), _TN)
              for h, i in enumerate(ids)]
    for h in range(N_HEADS):
        st_ref[h] = st[h]

    yn = _head_norm(y_ref[...], bd, RWKV_LNX_EPS) * lnw_ref[...] + lnb_ref[...]
    bonus = _mm_ones_rhs(r * kmod * rk_ref[...], bd) * v
    o_ref[0] = (yn + bonus) * _silu(g)


def _rwkv(c, mu, w0, w2, a0, a2, k_k, k_a, r_k, lnx_w, lnx_b, ts, ch):
    b, s, n = c.shape
    row = lambda t: t.reshape(1, -1)
    vec = lambda m: pl.BlockSpec((1, m), lambda i, j: (0, 0))
    mat = lambda m: pl.BlockSpec((RWKV_LORA, m), lambda i, j: (0, 0))
    return pl.pallas_call(
        functools.partial(_rwkv_kernel, ts=ts, ch=ch),
        grid=(b, s // ts),
        in_specs=[pl.BlockSpec((1, ts, n), lambda i, j: (i, j, 0)),
                  vec(n), vec(MIX_W), mat(MIX_W), vec(MIX_W), mat(MIX_W), vec(MIX_W), vec(MIX_W), vec(MIX_W),
                  vec(MIX_W), vec(MIX_W)],
        out_specs=pl.BlockSpec((1, ts, MIX_W), lambda i, j: (i, j, 0)),
        out_shape=jax.ShapeDtypeStruct((b, s, MIX_W), F32),
        scratch_shapes=[pltpu.VMEM((8, n), F32),
                        pltpu.VMEM((N_HEADS, HEAD_DIM, HEAD_DIM), F32),
                        pltpu.VMEM((ts, MIX_W), F32)],
        compiler_params=_params("parallel", "arbitrary"),
        name="rwkv7",
    )(c, row(mu), row(w0), w2, row(a0), a2, row(k_k), row(k_a), row(r_k), row(lnx_w), row(lnx_b))


def _ret_tables(tc):
    log_g = np.log(1.0 - np.exp(np.linspace(math.log(1.0 / 32), math.log(1.0 / 512), N_HEADS)))
    j = np.arange(tc, dtype=np.float64)
    rel = j[:, None] - j[None, :]
    dmask = np.where(rel >= 0, np.exp(log_g[:, None, None] * np.maximum(rel, 0.0)), 0.0)
    qdec = np.repeat(np.exp(log_g[None, :] * (j[:, None] + 1.0)), HEAD_DIM, axis=1)
    kdec = np.repeat(np.exp(log_g[None, :] * (tc - 1.0 - j[:, None])), HEAD_DIM, axis=1)
    cdec = np.repeat(np.exp(log_g * tc), HEAD_DIM)[None, :]
    f = lambda t: jnp.asarray(t, F32)
    return f(dmask), f(qdec), f(kdec), f(cdec)


def _ret_kernel(c_ref, cos_ref, sin_ref, dm_ref, qd_ref, kd_ref, cd_ref, gn_ref, o_ref, r_ref, y_ref):
    @pl.when(pl.program_id(1) == 0)
    def _():
        r_ref[...] = jnp.zeros_like(r_ref)

    w = MIX_W
    c = c_ref[0]
    cos, sin = cos_ref[0], sin_ref[0]
    q = _rope(c[:, 0:w], cos, sin)
    k = _rope(c[:, w:2 * w], cos, sin) * (HEAD_DIM ** -0.5)
    v = c[:, 2 * w:3 * w]
    g = c[:, 3 * w:]
    kd = k * kd_ref[...]
    qd = qd_ref[...]
    cd = cd_ref[...]
    hs = range(N_HEADS)
    ls = [slice(h * HEAD_DIM, (h + 1) * HEAD_DIM) for h in hs]
    qh = [q[:, l].astype(BF16) for l in ls]
    vh = [v[:, l].astype(BF16) for l in ls]
    rs = [r_ref[h] for h in hs]
    att = [_mm(qh[h], k[:, ls[h]], _NT) * dm_ref[h] for h in hs]
    cross = [_mm(qh[h], rs[h]) * qd[:, ls[h]] for h in hs]
    for h in hs:
        y_ref[:, ls[h]] = _mm(att[h], vh[h]) + cross[h]
    for h in hs:
        r_ref[h] = rs[h] * cd[:, ls[h]] + _mm(kd[:, ls[h]], vh[h], _TN)
    bd = _head_ones(w)
    o_ref[0] = _head_norm(y_ref[...], bd, NORM_EPS) * gn_ref[...] * _silu(g)


def _retention(c, cos, sin, gn_w, tc):
    b, s, n = c.shape
    dmask, qdec, kdec, cdec = _ret_tables(tc)
    tok = lambda m: pl.BlockSpec((1, tc, m), lambda i, j: (i, j, 0))
    full = lambda shape: pl.BlockSpec(shape, lambda i, j: (0,) * len(shape))
    return pl.pallas_call(
        _ret_kernel,
        grid=(b, s // tc),
        in_specs=[tok(n), tok(MIX_W), tok(MIX_W), full((N_HEADS, tc, tc)), full((tc, MIX_W)), full((tc, MIX_W)),
                  full((1, MIX_W)), full((1, MIX_W))],
        out_specs=tok(MIX_W),
        out_shape=jax.ShapeDtypeStruct((b, s, MIX_W), F32),
        scratch_shapes=[pltpu.VMEM((N_HEADS, HEAD_DIM, HEAD_DIM), F32), pltpu.VMEM((tc, MIX_W), F32)],
        compiler_params=_params("parallel", "arbitrary"),
        name="retention",
    )(c, cos, sin, dmask, qdec, kdec, cdec, gn_w.reshape(1, MIX_W))


def _s5_tables(lam_re, lam_im, log_dt, b_re, b_im, c_re, c_im, tl):
    gn, p = lam_re.shape
    lr = jnp.minimum(lam_re.astype(F32), -1e-4)
    li = lam_im.astype(F32)
    dt = jnp.exp(log_dt.astype(F32))[:, None]
    mag = jnp.exp(lr * dt)
    ab_re, ab_im = mag * jnp.cos(li * dt), mag * jnp.sin(li * dt)
    den = lr * lr + li * li
    f_re = ((ab_re - 1.0) * lr + ab_im * li) / den
    f_im = (ab_im * lr - (ab_re - 1.0) * li) / den
    bb_re = f_re[..., None] * b_re - f_im[..., None] * b_im
    bb_im = f_re[..., None] * b_im + f_im[..., None] * b_re
    eye = jnp.eye(gn, dtype=F32)
    bd_in = lambda bb: jnp.einsum('gpc,gh->gchp', bb, eye).reshape(gn * S5_GROUP, gn * p)
    w_in = jnp.concatenate([bd_in(bb_re), bd_in(bb_im)], axis=1)
    bd_out = lambda cc: jnp.einsum('gcp,gh->gphc', cc, eye).reshape(gn * p, gn * S5_GROUP)
    w_out = jnp.concatenate([bd_out(c_re), -bd_out(c_im)], axis=0)
    def powers(n):
        n = n.astype(F32)[:, None, None]
        m = jnp.exp(n * (lr * dt)[None])
        ang = n * (li * dt)[None]
        return (m * jnp.cos(ang)).reshape(-1, gn * p), (m * jnp.sin(ang)).reshape(-1, gn * p)
    steps = 2 ** jnp.arange(int(math.log2(tl)))
    dbl_re, dbl_im = powers(steps)
    car_re, car_im = powers(jnp.arange(1, tl + 1))
    return w_in, w_out, dbl_re, dbl_im, car_re, car_im


def _s5_kernel(c_ref, win_ref, wout_ref, dr_ref, di_ref, cr_ref, ci_ref, dskip_ref, wglu_ref, o_ref,
               xr_ref, xi_ref, *, tl):
    @pl.when(pl.program_id(1) == 0)
    def _():
        xr_ref[...] = jnp.zeros_like(xr_ref)
        xi_ref[...] = jnp.zeros_like(xi_ref)

    ns = S5_GROUPS * S5_STATE
    c = c_ref[0]
    u, g = c[:, :MIX_W], c[:, MIX_W:]
    bu = _mm(u, win_ref[...])
    xr, xi = bu[:, :ns], bu[:, ns:]
    row = lax.broadcasted_iota(I32, (tl, ns), 0)
    for j in range(int(math.log2(tl))):
        d = 1 << j
        ar, ai = dr_ref[j:j + 1, :], di_ref[j:j + 1, :]
        sr = jnp.where(row >= d, pltpu.roll(xr, d, axis=0), 0.0)
        si = jnp.where(row >= d, pltpu.roll(xi, d, axis=0), 0.0)
        xr, xi = xr + ar * sr - ai * si, xi + ar * si + ai * sr
    pr, pi = cr_ref[...], ci_ref[...]
    x0r, x0i = xr_ref[0:1, :], xi_ref[0:1, :]
    xr, xi = xr + pr * x0r - pi * x0i, xi + pr * x0i + pi * x0r
    xr_ref[0:1, :] = xr[tl - 1:tl, :]
    xi_ref[0:1, :] = xi[tl - 1:tl, :]
    y = _mm(jnp.concatenate([xr, xi], axis=-1), wout_ref[...]) + dskip_ref[...] * u
    y = y * (0.5 * (1.0 + jnp.tanh(math.sqrt(2.0 / math.pi) * (y + 0.044715 * (y * y * y)))))
    y = y * _sigmoid(_mm(y, wglu_ref[...]))
    o_ref[0] = y * _silu(g)


def _s5(c, lam_re, lam_im, log_dt, b_re, b_im, c_re, c_im, d_skip, w_glu, tl):
    b, s, n = c.shape
    ns = S5_GROUPS * S5_STATE
    w_in, w_out, dbl_re, dbl_im, car_re, car_im = _s5_tables(lam_re, lam_im, log_dt, b_re, b_im, c_re, c_im, tl)
    nd = dbl_re.shape[0]
    tok = lambda m: pl.BlockSpec((1, tl, m), lambda i, j: (i, j, 0))
    full = lambda shape: pl.BlockSpec(shape, lambda i, j: (0,) * len(shape))
    return pl.pallas_call(
        functools.partial(_s5_kernel, tl=tl),
        grid=(b, s // tl),
        in_specs=[tok(n), full((MIX_W, 2 * ns)), full((2 * ns, MIX_W)), full((nd, ns)), full((nd, ns)),
                  full((tl, ns)), full((tl, ns)), full((1, MIX_W)), full((MIX_W, MIX_W))],
        out_specs=tok(MIX_W),
        out_shape=jax.ShapeDtypeStruct((b, s, MIX_W), F32),
        scratch_shapes=[pltpu.VMEM((8, ns), F32), pltpu.VMEM((8, ns), F32)],
        compiler_params=_params("parallel", "arbitrary"),
        name="s5",
    )(c, w_in.astype(BF16), w_out.astype(BF16), dbl_re, dbl_im, car_re, car_im, d_skip.reshape(1, MIX_W),
      w_glu.astype(BF16))


def _xatt_kernel(c_ref, kv_ref, o_ref, y_ref):
    w = MIX_W
    c = c_ref[0]
    q, g = c[:, :w], c[:, w:]
    kv = kv_ref[0]
    hs = range(N_HEADS)
    ls = [slice(h * HEAD_DIM, (h + 1) * HEAD_DIM) for h in hs]
    lg = [_mm(q[:, l], kv[:, l], _NT) * (HEAD_DIM ** -0.5) for l in ls]
    p = [jnp.exp(x - jnp.max(x, axis=-1, keepdims=True)) for x in lg]
    for h in hs:
        y_ref[:, ls[h]] = (_mm(p[h], kv[:, w + h * HEAD_DIM:w + (h + 1) * HEAD_DIM])
                           / jnp.sum(p[h], axis=-1, keepdims=True))
    o_ref[0] = y_ref[...] * _silu(g)


def _xatt(c, mem_kv, tq):
    b, s, n = c.shape
    nm = mem_kv.shape[1]
    return pl.pallas_call(
        _xatt_kernel,
        grid=(b, s // tq),
        in_specs=[pl.BlockSpec((1, tq, n), lambda i, j: (i, j, 0)),
                  pl.BlockSpec((1, nm, 2 * MIX_W), lambda i, j: (i, 0, 0))],
        out_specs=pl.BlockSpec((1, tq, MIX_W), lambda i, j: (i, j, 0)),
        out_shape=jax.ShapeDtypeStruct((b, s, MIX_W), F32),
        scratch_shapes=[pltpu.VMEM((tq, MIX_W), F32)],
        compiler_params=_params("parallel", "parallel"),
        name="mem_xatt",
    )(c, mem_kv)


def _dsa_prep_kernel(c_ref, ikw_ref, cos_ref, sin_ref, q_ref, k_ref, v_ref, iq_ref, ik_ref):
    w = MIX_W
    c = c_ref[0]
    cos, sin = cos_ref[0], sin_ref[0]
    q = _rope(c[:, 0:w], cos, sin) * (HEAD_DIM ** -0.5)
    k = _rope(c[:, w:2 * w], cos, sin)
    v = c[:, 2 * w:3 * w]
    iq = _rope(c[:, 3 * w:4 * w], cos, sin) * (IDX_DIM ** -0.5)
    ik = _rope(ikw_ref[0], cos[:, :LANES], sin[:, :LANES])
    one_col = jnp.where(lax.broadcasted_iota(I32, (c.shape[0], HEAD_DIM), 1) == 0, 1.0, 0.0)
    for h in range(N_HEADS):
        ls = slice(h * HEAD_DIM, (h + 1) * HEAD_DIM)
        q_ref[0, h] = q[:, ls].astype(BF16)
        k_ref[0, h] = k[:, ls].astype(BF16)
        v_ref[0, h] = jnp.concatenate([v[:, ls], one_col], axis=-1).astype(BF16)
        iq_ref[0, h] = iq[:, ls].astype(BF16)
    ik_ref[0] = ik[:, :IDX_DIM].astype(BF16)


def _dsa_prep(c_main, c_ikw, cos, sin, tp):
    b, s, _ = c_main.shape
    tok = lambda m: pl.BlockSpec((1, tp, m), lambda i, j: (i, j, 0))
    hm = lambda m: pl.BlockSpec((1, N_HEADS, tp, m), lambda i, j: (i, 0, j, 0))
    hm_shape = lambda m: jax.ShapeDtypeStruct((b, N_HEADS, s, m), BF16)
    return pl.pallas_call(
        _dsa_prep_kernel,
        grid=(b, s // tp),
        in_specs=[pl.BlockSpec((1, tp, 4 * MIX_W), lambda i, j: (i, j, 0)), tok(LANES), tok(MIX_W), tok(MIX_W)],
        out_specs=[hm(HEAD_DIM), hm(HEAD_DIM), hm(2 * HEAD_DIM), hm(HEAD_DIM), tok(IDX_DIM)],
        out_shape=[hm_shape(HEAD_DIM), hm_shape(HEAD_DIM), hm_shape(2 * HEAD_DIM), hm_shape(HEAD_DIM),
                   jax.ShapeDtypeStruct((b, s, IDX_DIM), BF16)],
        compiler_params=_params("parallel", "parallel"),
        name="dsa_prep",
    )(c_main, c_ikw, cos, sin)


def _dsa_kernel(q_ref, iq_ref, ikw_ref, g_ref, k_ref, v_ref, ik_ref, tri_ref, o_ref, key_ref, planes_ref, m_ref,
                acc_ref, *, qb, kc, n_sel, n_tile):
    q0 = pl.program_id(1) * qb
    nkc = (q0 + qb + kc - 1) // kc
    qidx = q0 + lax.broadcasted_iota(I32, (qb, kc), 0)
    lane = lax.broadcasted_iota(I32, (qb, kc), 1)
    wide = lambda t: jnp.concatenate([t] * (kc // LANES), axis=1)
    tw = tri_ref.shape[0]
    ones_k = jnp.ones((tw, LANES), BF16)
    chunk = lambda c: pl.ds(pl.multiple_of(c * kc, kc), kc)
    iw = ikw_ref[0] * (IDX_HEADS ** -0.5)
    iws = [iw[:, IDX_DIM + h:IDX_DIM + h + 1] for h in range(IDX_HEADS)]

    def score_body(c, carry):
        ikc = ik_ref[0, chunk(c), :]
        s = [lax.dot_general(iq_ref[0, h], ikc, _NT, preferred_element_type=F32) for h in range(IDX_HEADS)]
        acc = jnp.maximum(s[0], 0.0) * iws[0]
        for h in range(1, IDX_HEADS):
            acc = acc + jnp.maximum(s[h], 0.0) * iws[h]
        acc = jnp.where(acc == 0.0, 0.0, acc)
        bits = pltpu.bitcast(acc, I32)
        key = jnp.where(bits < 0, bits ^ 0x7FFFFFFF, bits)
        key_ref[:, chunk(c)] = jnp.where(c * kc + lane <= qidx, key, INT_MIN)
        return carry

    lax.fori_loop(0, nkc, score_body, 0)

    def fill_body(c, carry):
        key_ref[:, chunk(c)] = jnp.full((qb, kc), INT_MIN, I32)
        return carry

    lax.fori_loop(nkc, n_tile * LANES // kc, fill_body, 0)

    def plane_body(gidx, carry):
        rows = pl.ds(pl.multiple_of(gidx * 8, 8), 8)
        tiles = [key_ref[rows, i * LANES:(i + 1) * LANES] ^ INT_MIN for i in range(n_tile)]
        tiles += [jnp.zeros((8, LANES), I32)] * (32 - n_tile)
        for p, plane in enumerate(_bit_transpose32(tiles)):
            planes_ref[p, rows, :] = plane
        return carry

    lax.fori_loop(0, qb // 8, plane_body, 0)

    ones_l = jnp.ones((LANES, LANES), BF16)

    def bit_body(t, carry):
        alive, above, vbits = carry
        p = 2 * t
        hi, lo = planes_ref[p], planes_ref[p + 1]
        a1 = alive & hi
        a0 = alive ^ a1
        o11 = a1 & lo
        o10 = a1 ^ o11
        o01 = a0 & lo
        pc = jnp.concatenate([lax.population_count(o).astype(F32).astype(BF16) for o in (o11, o10, o01)], axis=0)
        cnt = jnp.dot(pc, ones_l, preferred_element_type=F32)
        s1 = above + cnt[:qb]
        s2 = s1 + cnt[qb:2 * qb]
        s3 = s2 + cnt[2 * qb:]
        t11, t10, t01 = s1 >= n_sel, s2 >= n_sel, s3 >= n_sel
        alive = jnp.where(t11, o11, jnp.where(t10, o10, jnp.where(t01, o01, a0 ^ o01)))
        above = jnp.where(t11, above, jnp.where(t10, s1, jnp.where(t01, s2, s3)))
        digit = jnp.where(t11, 3, jnp.where(t10, 2, jnp.where(t01, 1, 0)))
        return alive, above, vbits | jnp.left_shift(digit, 30 - p)

    _, above, vbits = lax.fori_loop(
        0, 16, bit_body, (jnp.full((qb, LANES), -1, I32), jnp.zeros((qb, LANES), F32), jnp.zeros((qb, LANES), I32)))
    vs = wide(jnp.maximum(vbits ^ INT_MIN, INT_MIN + 1))
    need = wide(n_sel - above)

    def bias_body(c, run):
        kv = key_ref[:, chunk(c)]
        tied = kv == vs
        tied_b = jnp.where(tied, 1.0, 0.0).astype(BF16)
        ranks = []
        for j in range(kc // tw):
            part = tied_b[:, j * tw:(j + 1) * tw]
            ranks.append(jnp.concatenate([run] * (tw // LANES), axis=1)
                         + jnp.dot(part, tri_ref[...], preferred_element_type=F32))
            run = run + jnp.dot(part, ones_k, preferred_element_type=F32)
        rank = jnp.concatenate(ranks, axis=1)
        bias = jnp.where(kv > vs, 0.0, jnp.where(tied, jnp.where(rank <= need, 0.0, -1e30), -1e30))
        key_ref[:, chunk(c)] = pltpu.bitcast(bias, I32)
        return run

    lax.fori_loop(0, nkc, bias_body, jnp.zeros((qb, LANES), F32))

    m_ref[...] = jnp.full(m_ref.shape, -1e30, F32)
    acc_ref[...] = jnp.zeros(acc_ref.shape, F32)

    def att_body(c, carry):
        bias = pltpu.bitcast(key_ref[:, chunk(c)], F32)
        hs = range(N_HEADS)
        lg = [lax.dot_general(q_ref[0, h], k_ref[0, h, chunk(c), :], _NT, preferred_element_type=F32) + bias
              for h in hs]
        m_old = [m_ref[h] for h in hs]
        m_new = [jnp.maximum(m_old[h], jnp.broadcast_to(jnp.max(lg[h], axis=-1, keepdims=True), (qb, LANES)))
                 for h in hs]
        p = [jnp.exp(lg[h] - wide(m_new[h])).astype(BF16) for h in hs]
        for h in hs:
            acc_ref[h] = jnp.exp(m_old[h] - m_new[h]) * acc_ref[h] + jnp.dot(p[h], v_ref[0, h, chunk(c), :],
                                                                             preferred_element_type=F32)
            m_ref[h] = m_new[h]
        return carry

    lax.fori_loop(0, nkc, att_body, 0)
    y = jnp.concatenate([acc_ref[h][:, :HEAD_DIM] / acc_ref[h][:, HEAD_DIM:HEAD_DIM + 1] for h in range(N_HEADS)],
                        axis=-1)
    o_ref[0] = y * _silu(g_ref[0])


def _dsa(c_main, c_ikw, cos, sin, qb, kc, tp):
    b, s, _ = c_main.shape
    q, k, v, iq, ik = _dsa_prep(c_main, c_ikw, cos, sin, tp)
    n_sel = min(TOPK_MAX, s // 4)
    n_tile = s // LANES
    assert n_tile <= 32 and s % kc == 0, "one key per bit of a 32-bit plane word"
    tw = min(2 * LANES, kc)
    tri = jnp.asarray(np.triu(np.ones((tw, tw), np.float32)), BF16)
    qblk = pl.BlockSpec((1, N_HEADS, qb, HEAD_DIM), lambda i, j: (i, 0, j, 0))
    return pl.pallas_call(
        functools.partial(_dsa_kernel, qb=qb, kc=kc, n_sel=n_sel, n_tile=n_tile),
        grid=(b, s // qb),
        in_specs=[qblk, qblk,
                  pl.BlockSpec((1, qb, LANES), lambda i, j: (i, j, 0)),
                  pl.BlockSpec((1, qb, MIX_W), lambda i, j: (i, j, 4)),
                  pl.BlockSpec((1, N_HEADS, s, HEAD_DIM), lambda i, j: (i, 0, 0, 0)),
                  pl.BlockSpec((1, N_HEADS, s, 2 * HEAD_DIM), lambda i, j: (i, 0, 0, 0)),
                  pl.BlockSpec((1, s, IDX_DIM), lambda i, j: (i, 0, 0)),
                  pl.BlockSpec((tw, tw), lambda i, j: (0, 0))],
        out_specs=pl.BlockSpec((1, qb, MIX_W), lambda i, j: (i, j, 0)),
        out_shape=jax.ShapeDtypeStruct((b, s, MIX_W), F32),
        scratch_shapes=[pltpu.VMEM((qb, s), I32), pltpu.VMEM((32, qb, LANES), I32),
                        pltpu.VMEM((N_HEADS, qb, LANES), F32),
                        pltpu.VMEM((N_HEADS, qb, LANES), F32)],
        compiler_params=_params("parallel", "arbitrary"),
        name="dsa",
    )(q, iq, c_ikw, c_main, k, v, ik, tri)


def _merge_kernel(x_ref, gpre_ref, gpost_ref, wg_ref, wb_ref, wo_ref, y0, y1, y2, y3, y4, o_ref):
    x = x_ref[...]
    hb = _rms(x, gpre_ref[...]).astype(BF16)
    merged = None
    for i, y_ref in enumerate((y0, y1, y2, y3, y4)):
        gate = _sigmoid(jnp.dot(hb, wg_ref[:, i * D_MODEL:(i + 1) * D_MODEL], preferred_element_type=F32))
        term = gate * jnp.dot(y_ref[...].astype(BF16), wb_ref[i], preferred_element_type=F32)
        merged = term if merged is None else merged + term
    out = jnp.dot(merged.astype(BF16), wo_ref[...], preferred_element_type=F32)
    o_ref[...] = x + _rms(out, gpost_ref[...])


def _merge(x2d, g_pre, g_post, w_gate, w_branch, w_out, ys, tm):
    t, d = x2d.shape
    tok = lambda m: pl.BlockSpec((tm, m), lambda i: (i, 0))
    full = lambda shape: pl.BlockSpec(shape, lambda i: (0,) * len(shape))
    once = lambda shape: pl.BlockSpec(shape, lambda i: (0,) * len(shape), pipeline_mode=pl.Buffered(1))
    return pl.pallas_call(
        _merge_kernel,
        grid=(t // tm,),
        in_specs=[tok(d), full((1, d)), full((1, d)), once((d, GATE_COLS)), once((N_BRANCH, MIX_W, d)),
                  once((d, d))] + [tok(MIX_W)] * N_BRANCH,
        out_specs=tok(d),
        out_shape=jax.ShapeDtypeStruct((t, d), F32),
        compiler_params=_params("parallel"),
        name="merge",
    )(x2d, g_pre.reshape(1, d), g_post.reshape(1, d), w_gate, w_branch, w_out, *ys)


def kernel(x, mem, positions, norm_pre, norm_post, norm_mem, w_in, rwkv_mu, rwkv_w0, rwkv_w2, rwkv_a0, rwkv_a2, rwkv_k_k, rwkv_k_a, rwkv_r_k, rwkv_lnx_w, rwkv_lnx_b, ret_gn_w, s5_lam_re, s5_lam_im, s5_log_dt, s5_b_re, s5_b_im, s5_c_re, s5_c_im, s5_d, s5_w_glu, w_mem_kv, w_branch, w_out):
    b, s, d = x.shape
    t = b * s
    n_mem = mem.shape[1]
    depth = w_in.shape[0]
    tm = min(512, t)
    cos, sin = _rope_tables(positions, min(512, s))
    o_dsa = RWKV_COLS
    o_ret = o_dsa + DSA_COLS
    o_s5 = o_ret + RET_COLS
    o_x = o_s5 + S5_COLS
    o_gate = o_x + XATT_COLS
    widths = (RWKV_COLS, 5 * MIX_W, LANES, RET_COLS, S5_COLS, XATT_COLS)
    for l in range(depth):
        w = w_in[l]
        o_idx = o_dsa + 4 * MIX_W
        o_g = o_idx + IDX_DIM + IDX_HEADS
        w_cat = _to_bf16(jnp.concatenate([
            w[:, :RWKV_COLS],
            w[:, o_dsa:o_idx], w[:, o_g:o_ret],
            w[:, o_idx:o_g], jnp.zeros((d, LANES - IDX_DIM - IDX_HEADS), w.dtype),
            w[:, o_ret:o_gate]], axis=1))
        x2 = x.reshape(t, d)
        c_rwkv, c_dsa, c_ikw, c_ret, c_s5, c_x = _norm_proj(x2, norm_pre[l], w_cat, widths, tm)
        (mem_kv,) = _norm_proj(mem.reshape(b * n_mem, d), norm_mem[l], _to_bf16(w_mem_kv[l]), (2 * MIX_W,),
                               min(256, b * n_mem))
        r3 = lambda a: a.reshape(b, s, a.shape[-1])
        y_rwkv = _rwkv(r3(c_rwkv), rwkv_mu[l], rwkv_w0[l], rwkv_w2[l], rwkv_a0[l], rwkv_a2[l], rwkv_k_k[l],
                       rwkv_k_a[l], rwkv_r_k[l], rwkv_lnx_w[l], rwkv_lnx_b[l], ts=min(256, s), ch=64)
        y_dsa = _dsa(r3(c_dsa), r3(c_ikw), cos, sin, qb=min(512, s), kc=min(512, s), tp=min(512, s))
        y_ret = _retention(r3(c_ret), cos, sin, ret_gn_w[l], tc=min(256, s))
        y_s5 = _s5(r3(c_s5), s5_lam_re[l], s5_lam_im[l], s5_log_dt[l], s5_b_re[l], s5_b_im[l], s5_c_re[l],
                   s5_c_im[l], s5_d[l], s5_w_glu[l], tl=min(128, s))
        y_x = _xatt(r3(c_x), mem_kv.reshape(b, n_mem, 2 * MIX_W), tq=min(512, s))
        ys = [y.reshape(t, MIX_W) for y in (y_rwkv, y_dsa, y_ret, y_s5, y_x)]
        x = _merge(x2, norm_pre[l], norm_post[l], _to_bf16(w[:, o_gate:]), _to_bf16(w_branch[l]),
                   _to_bf16(w_out[l]), ys, tm).reshape(b, s, d)
    return x
```
